```python
import math
import jax, jax.numpy as jnp
from jax import lax
import numpy as np

D_MODEL = 2048
BATCH = 4
SEQ = 8192
DEPTH = 4

CHUNK = 64
N_MIXERS = 3
EPS = 1e-6
MASK_VALUE = -1e30

ATT_HEADS = 16
ATT_HEAD_DIM = D_MODEL // ATT_HEADS
LEFT_CHUNKS = 8
BAND_LEFT = LEFT_CHUNKS * CHUNK
BAND = (LEFT_CHUNKS + 1) * CHUNK
MAX_REL = 256
NUM_REL = (CHUNK - 1) + MAX_REL + 1

POOL_WINDOWS = (2, 4, 8, 16)
POOL_GROUPS = len(POOL_WINDOWS)
POOL_DIM = D_MODEL // POOL_GROUPS

GDN_K_HEADS = 16
GDN_V_HEADS = 32
GDN_K_DIM = D_MODEL // GDN_K_HEADS
GDN_V_DIM = D_MODEL // GDN_K_HEADS
GDN_KEY = GDN_K_HEADS * GDN_K_DIM
GDN_VAL = GDN_V_HEADS * GDN_V_DIM
GDN_CONV = 4
GDN_CONV_CH = 2 * GDN_KEY + GDN_VAL
GDN_IN = GDN_CONV_CH + GDN_VAL + 2 * GDN_V_HEADS

D_FF = 128 * ((8 * D_MODEL // 3 + 127) // 128)
FFN_CONV = 3

N_ATT_LAYERS = (DEPTH + 2) // 3
N_POOL_LAYERS = (DEPTH + 1) // 3
N_GDN_LAYERS = DEPTH // 3

kernel_name = "hybrid_chunk_causal_encoder"


def rms_norm(x, gain):
    xf = x.astype(jnp.float32)
    y = xf * lax.rsqrt(jnp.mean(xf * xf, axis=-1, keepdims=True) + EPS)
    return (y * gain.astype(jnp.float32)).astype(x.dtype)


def l2_norm(x):
    xf = x.astype(jnp.float32)
    return xf * lax.rsqrt(jnp.sum(xf * xf, axis=-1, keepdims=True) + EPS)


def causal_depthwise_conv(x, w):
    k = w.shape[0]
    s = x.shape[1]
    xp = jnp.pad(x, ((0, 0), (k - 1, 0), (0, 0)))
    out = xp[:, 0:s] * w[0]
    for j in range(1, k):
        out = out + xp[:, j:j + s] * w[j]
    return out


def chunk_band_attention(h, w_qkv, q_gain, k_gain, rel_bias, w_o):
    b, s, d = h.shape
    nc = s // CHUNK
    qkv = h @ w_qkv
    q = qkv[..., :d].reshape(b, s, ATT_HEADS, ATT_HEAD_DIM)
    k = qkv[..., d:2 * d].reshape(b, s, ATT_HEADS, ATT_HEAD_DIM)
    v = qkv[..., 2 * d:].reshape(b, s, ATT_HEADS, ATT_HEAD_DIM)
    q = rms_norm(q, q_gain)
    k = rms_norm(k, k_gain)
    k_pad = jnp.pad(k, ((0, 0), (BAND_LEFT, 0), (0, 0), (0, 0)))
    v_pad = jnp.pad(v, ((0, 0), (BAND_LEFT, 0), (0, 0), (0, 0)))
    rel = BAND_LEFT + jnp.arange(CHUNK)[:, None] - jnp.arange(BAND)[None, :]
    rel_idx = jnp.clip(rel, -(CHUNK - 1), MAX_REL) + (CHUNK - 1)
    bias = rel_bias.astype(jnp.float32)[:, rel_idx]
    scale = ATT_HEAD_DIM ** -0.5
    q_chunks = q.reshape(b, nc, CHUNK, ATT_HEADS, ATT_HEAD_DIM).transpose(1, 0, 2, 3, 4)

    def per_chunk(args):
        c, qc = args
        kb = lax.dynamic_slice_in_dim(k_pad, c * CHUNK, BAND, axis=1)
        vb = lax.dynamic_slice_in_dim(v_pad, c * CHUNK, BAND, axis=1)
        sc = jnp.einsum('bqhd,bkhd->bhqk', qc, kb,
                        preferred_element_type=jnp.float32) * scale + bias
        valid = (c * CHUNK - BAND_LEFT + jnp.arange(BAND)) >= 0
        sc = jnp.where(valid[None, None, None, :], sc, MASK_VALUE)
        p = jax.nn.softmax(sc, axis=-1).astype(vb.dtype)
        return jnp.einsum('bhqk,bkhd->bqhd', p, vb)

    o = lax.map(per_chunk, (jnp.arange(nc, dtype=jnp.int32), q_chunks))
    o = o.transpose(1, 0, 2, 3, 4).reshape(b, s, d)
    return o @ w_o


def multiscale_pool_mixer(h, pool_w, pool_scale):
    b, s, d = h.shape
    hf = h.astype(jnp.float32)
    cs = jnp.cumsum(jnp.pad(hf, ((0, 0), (1, 0), (0, 0))), axis=1)
    pos = jnp.arange(s)
    groups = []
    for g, w in enumerate(POOL_WINDOWS):
        csg = cs[..., g * POOL_DIM:(g + 1) * POOL_DIM]
        upper = csg[:, 1:]
        lower = jnp.pad(csg[:, :s + 1 - w], ((0, 0), (w - 1, 0), (0, 0)))
        count = jnp.minimum(pos + 1, w).astype(jnp.float32)[None, :, None]
        groups.append((upper - lower) / count - hf[..., g * POOL_DIM:(g + 1) * POOL_DIM])
    pooled = jnp.stack(groups, axis=2)
    y = jnp.einsum('bsgc,gce->bsge', pooled, pool_w.astype(jnp.float32)).reshape(b, s, d)
    return (y * pool_scale.astype(jnp.float32)).astype(h.dtype)


def gated_delta_rule(q, k, v, g, beta):
    b, s, h, dk = q.shape
    dv = v.shape[-1]
    nc = s // CHUNK

    def to_chunks(t):
        return t.reshape(b, nc, CHUNK, h, -1).transpose(0, 3, 1, 2, 4)

    q, k, v = to_chunks(q), to_chunks(k), to_chunks(v)
    g = g.reshape(b, nc, CHUNK, h).transpose(0, 3, 1, 2)
    beta = beta.reshape(b, nc, CHUNK, h).transpose(0, 3, 1, 2)
    gc = jnp.cumsum(g, axis=-1)
    idx = jnp.arange(CHUNK)
    causal = idx[:, None] >= idx[None, :]
    strict = idx[:, None] > idx[None, :]
    diff = gc[..., :, None] - gc[..., None, :]
    decay = jnp.where(causal, jnp.exp(jnp.where(causal, diff, 0.0)), 0.0)
    kb = k * beta[..., None]
    vb = v * beta[..., None]
    a_strict = jnp.where(strict, jnp.einsum('bhncd,bhnjd->bhncj', kb, k) * decay, 0.0)
    eye = jnp.eye(CHUNK, dtype=jnp.float32)
    rhs = jnp.concatenate([vb, kb * jnp.exp(gc)[..., None]], axis=-1)
    sol = lax.linalg.triangular_solve(a_strict + eye, rhs, left_side=True, lower=True,
                                      unit_diagonal=True)
    u = sol[..., :dv]
    w = sol[..., dv:]
    attn = jnp.einsum('bhncd,bhnjd->bhncj', q, k) * decay
    qg = q * jnp.exp(gc)[..., None]
    k_state = k * jnp.exp(gc[..., -1:] - gc)[..., None]
    chunk_decay = jnp.exp(gc[..., -1])
    xs = tuple(jnp.moveaxis(t, 2, 0) for t in (qg, attn, u, w, k_state, chunk_decay))

    def step(state, inp):
        qg_c, attn_c, u_c, w_c, ks_c, dec_c = inp
        v_new = u_c - jnp.einsum('bhcd,bhde->bhce', w_c, state)
        o_c = (jnp.einsum('bhcd,bhde->bhce', qg_c, state)
               + jnp.einsum('bhcj,bhje->bhce', attn_c, v_new))
        state = state * dec_c[..., None, None] + jnp.einsum('bhcd,bhce->bhde', ks_c, v_new)
        return state, o_c

    state0 = jnp.zeros((b, h, dk, dv), jnp.float32)
    _, o = lax.scan(step, state0, xs)
    return o.transpose(1, 0, 3, 2, 4).reshape(b, s, h, dv)


def gated_deltanet_mixer(h, w_in, conv_w, a_log, dt_bias, o_gain, w_o):
    b, s, _ = h.shape
    proj = h @ w_in
    qkv = jax.nn.silu(causal_depthwise_conv(proj[..., :GDN_CONV_CH], conv_w))
    gate = proj[..., GDN_CONV_CH:GDN_CONV_CH + GDN_VAL]
    a = proj[..., GDN_CONV_CH + GDN_VAL:GDN_CONV_CH + GDN_VAL + GDN_V_HEADS]
    bt = proj[..., GDN_CONV_CH + GDN_VAL + GDN_V_HEADS:]
    q = qkv[..., :GDN_KEY].reshape(b, s, GDN_K_HEADS, GDN_K_DIM)
    k = qkv[..., GDN_KEY:2 * GDN_KEY].reshape(b, s, GDN_K_HEADS, GDN_K_DIM)
    v = qkv[..., 2 * GDN_KEY:].reshape(b, s, GDN_V_HEADS, GDN_V_DIM).astype(jnp.float32)
    q = l2_norm(q) * (GDN_K_DIM ** -0.5)
    k = l2_norm(k)
    rep = GDN_V_HEADS // GDN_K_HEADS
    q = jnp.repeat(q, rep, axis=2)
    k = jnp.repeat(k, rep, axis=2)
    beta = jax.nn.sigmoid(bt.astype(jnp.float32))
    g = -jnp.exp(a_log.astype(jnp.float32)) * jax.nn.softplus(
        a.astype(jnp.float32) + dt_bias.astype(jnp.float32))
    o = gated_delta_rule(q, k, v, g, beta)
    o = rms_norm(o, o_gain) * jax.nn.silu(
        gate.astype(jnp.float32).reshape(b, s, GDN_V_HEADS, GDN_V_DIM))
    return o.reshape(b, s, GDN_VAL).astype(h.dtype) @ w_o


def conv_ffn(h, w_up, conv_w, w_down):
    up = h @ w_up
    u = causal_depthwise_conv(up[..., :D_FF], conv_w)
    return (jax.nn.silu(u) * up[..., D_FF:]) @ w_down


def _normal(k, shape, scale):
    return scale * jax.random.normal(k, shape, jnp.float32)


def setup_inputs(seed: int = 0) -> dict:
    key = jax.random.key(seed)
    ks = jax.random.split(key, 24)
    d = D_MODEL
    dt = jnp.exp(jax.random.uniform(ks[14], (N_GDN_LAYERS, GDN_V_HEADS), jnp.float32,
                                    minval=math.log(1e-3), maxval=math.log(1e-1)))
    return {
        "x": _normal(ks[0], (BATCH, SEQ, d), 1.0),
        "mix_norm": 1.0 + _normal(ks[1], (DEPTH, d), 0.02),
        "ffn_norm": 1.0 + _normal(ks[2], (DEPTH, d), 0.02),
        "att_w_qkv": _normal(ks[3], (N_ATT_LAYERS, d, 3 * d), d ** -0.5),
        "att_q_gain": 1.0 + _normal(ks[4], (N_ATT_LAYERS, ATT_HEAD_DIM), 0.02),
        "att_k_gain": 1.0 + _normal(ks[5], (N_ATT_LAYERS, ATT_HEAD_DIM), 0.02),
        "att_rel_bias": _normal(ks[6], (N_ATT_LAYERS, ATT_HEADS, NUM_REL), 0.2),
        "att_w_o": _normal(ks[7], (N_ATT_LAYERS, d, d), d ** -0.5),
        "pool_w": _normal(ks[8], (N_POOL_LAYERS, POOL_GROUPS, POOL_DIM, POOL_DIM),
                           POOL_DIM ** -0.5),
        "pool_scale": 1.0 + _normal(ks[9], (N_POOL_LAYERS, d), 0.1),
        "gdn_w_in": _normal(ks[10], (N_GDN_LAYERS, d, GDN_IN), d ** -0.5),
        "gdn_conv": _normal(ks[11], (N_GDN_LAYERS, GDN_CONV, GDN_CONV_CH), GDN_CONV ** -0.5),
        "gdn_a_log": jnp.log(jax.random.uniform(ks[12], (N_GDN_LAYERS, GDN_V_HEADS),
                                                 jnp.float32, minval=1.0, maxval=16.0)),
        "gdn_dt_bias": dt + jnp.log(-jnp.expm1(-dt)),
        "gdn_o_gain": 1.0 + _normal(ks[13], (N_GDN_LAYERS, GDN_V_DIM), 0.02),
        "gdn_w_o": _normal(ks[15], (N_GDN_LAYERS, GDN_VAL, d), GDN_VAL ** -0.5),
        "ffn_w_up": _normal(ks[16], (DEPTH, d, 2 * D_FF), d ** -0.5),
        "ffn_conv": _normal(ks[17], (DEPTH, FFN_CONV, D_FF), FFN_CONV ** -0.5),
        "ffn_w_down": _normal(ks[18], (DEPTH, D_FF, d), D_FF ** -0.5),
    }


def reference(x, mix_norm, ffn_norm, att_w_qkv, att_q_gain, att_k_gain, att_rel_bias,
              att_w_o, pool_w, pool_scale, gdn_w_in, gdn_conv, gdn_a_log, gdn_dt_bias,
              gdn_o_gain, gdn_w_o, ffn_w_up, ffn_conv, ffn_w_down):
    for i in range(DEPTH):
        kind = i % N_MIXERS
        j = i // N_MIXERS
        h = rms_norm(x, mix_norm[i])
        if kind == 0:
            y = chunk_band_attention(h, att_w_qkv[j], att_q_gain[j], att_k_gain[j],
                                     att_rel_bias[j], att_w_o[j])
        elif kind == 1:
            y = multiscale_pool_mixer(h, pool_w[j], pool_scale[j])
        else:
            y = gated_deltanet_mixer(h, gdn_w_in[j], gdn_conv[j], gdn_a_log[j],
                                     gdn_dt_bias[j], gdn_o_gain[j], gdn_w_o[j])
        x = x + y
        h = rms_norm(x, ffn_norm[i])
        x = x + conv_ffn(h, ffn_w_up[i], ffn_conv[i], ffn_w_down[i])
    return x
```

```python
import functools

import numpy as np
import jax
import jax.numpy as jnp
from jax import lax
from jax.experimental import pallas as pl
from jax.experimental.pallas import tpu as pltpu

F32 = jnp.float32
BF16 = jnp.bfloat16

CHUNK = 64
EPS = 1e-6
MASK_VALUE = -1e30
N_MIXERS = 3

ATT_HEADS = 16
LEFT_CHUNKS = 8
MAX_REL = 256

POOL_WINDOWS = (2, 4, 8, 16)

GDN_K_HEADS = 16
GDN_V_HEADS = 32

LANES = 128
SUBLANES = 8
V7X_VMEM_BYTES = 64 * 1024 * 1024

TM = 512
TN = 512
ATT_ROWS = 256
ATT_HEAD_GROUP = 4
GDN_CHUNKS_PER_STEP = 8
INV_PRECISION = lax.Precision.HIGHEST


def _compiler_params(semantics, vmem_bytes):
    assert vmem_bytes < V7X_VMEM_BYTES
    return pltpu.CompilerParams(dimension_semantics=semantics, vmem_limit_bytes=vmem_bytes)


def _rms_rows(x, gain):
    ms = jnp.mean(x * x, axis=-1, keepdims=True)
    return x * lax.rsqrt(ms + EPS) * gain


def _silu(x):
    return x * jax.nn.sigmoid(x)


def _dot(a, b):
    return jnp.dot(a, b, preferred_element_type=F32)


def _dot_nt(a, b):
    return lax.dot_general(a, b, (((1,), (1,)), ((), ())), preferred_element_type=F32)


def _causal_conv_tile(u, cw_ref, tail_ref, ext_ref, j, first):
    tm = u.shape[0]
    k = cw_ref.shape[0]
    prev = jnp.where(first, 0.0, tail_ref[j])
    ext_ref[0:SUBLANES, :] = prev
    ext_ref[SUBLANES:SUBLANES + tm, :] = u
    tail_ref[j] = u[tm - SUBLANES:, :]
    acc = u * cw_ref[k - 1:k, :]
    for s in range(1, k):
        acc = acc + ext_ref[pl.ds(SUBLANES - s, tm), :] * cw_ref[k - 1 - s:k - s, :]
    return acc


def _qkv_kernel(x_ref, g_ref, w_ref, qkg_ref, o_ref, h_ref, *, n_q_tiles, head_dim):
    j = pl.program_id(1)

    @pl.when(j == 0)
    def _():
        h_ref[...] = _rms_rows(x_ref[...], g_ref[...]).astype(BF16)

    y = _dot(h_ref[...], w_ref[...])
    kind = j // n_q_tiles

    @pl.when(kind < 2)
    def _():
        gain = qkg_ref[pl.ds(kind, 1), :]
        for hh in range(y.shape[1] // head_dim):
            sl = slice(hh * head_dim, (hh + 1) * head_dim)
            yh = y[:, sl]
            r = lax.rsqrt(jnp.mean(yh * yh, axis=-1, keepdims=True) + EPS)
            o_ref[:, sl] = (yh * r * gain).astype(o_ref.dtype)

    @pl.when(kind == 2)
    def _():
        o_ref[...] = y.astype(o_ref.dtype)


def _qkv_proj(x, gain, w, qk_gain, head_dim):
    m, d = x.shape
    n = w.shape[1]
    kern = functools.partial(_qkv_kernel, n_q_tiles=d // TN, head_dim=head_dim)
    return pl.pallas_call(
        kern,
        grid=(m // TM, n // TN),
        in_specs=[
            pl.BlockSpec((TM, d), lambda i, j: (i, 0)),
            pl.BlockSpec((1, d), lambda i, j: (0, 0)),
            pl.BlockSpec((d, TN), lambda i, j: (0, j)),
            pl.BlockSpec((SUBLANES, head_dim), lambda i, j: (0, 0)),
        ],
        out_specs=pl.BlockSpec((TM, TN), lambda i, j: (i, j)),
        out_shape=jax.ShapeDtypeStruct((m, n), BF16),
        scratch_shapes=[pltpu.VMEM((TM, d), BF16)],
        compiler_params=_compiler_params(("arbitrary", "arbitrary"), 40 * 1024 * 1024),
        name="qkv_proj",
    )(x, gain, w, qk_gain)


def _ffn_up_kernel(x_ref, g_ref, wu_ref, wg_ref, cw_ref, o_ref, h_ref, tail_ref, ext_ref,
                   *, tiles_per_seq):
    i = pl.program_id(0)
    j = pl.program_id(1)

    @pl.when(j == 0)
    def _():
        h_ref[...] = _rms_rows(x_ref[...], g_ref[...]).astype(BF16)

    h = h_ref[...]
    u = _dot(h, wu_ref[...])
    gate = _dot(h, wg_ref[...])
    first = (i % tiles_per_seq) == 0
    c = _causal_conv_tile(u, cw_ref, tail_ref, ext_ref, j, first)
    o_ref[...] = (_silu(c) * gate).astype(o_ref.dtype)


def _ffn_up(x, gain, wu, wg, cw, seq):
    m, d = x.shape
    n = wu.shape[1]
    kern = functools.partial(_ffn_up_kernel, tiles_per_seq=seq // TM)
    return pl.pallas_call(
        kern,
        grid=(m // TM, n // TN),
        in_specs=[
            pl.BlockSpec((TM, d), lambda i, j: (i, 0)),
            pl.BlockSpec((1, d), lambda i, j: (0, 0)),
            pl.BlockSpec((d, TN), lambda i, j: (0, j)),
            pl.BlockSpec((d, TN), lambda i, j: (0, j)),
            pl.BlockSpec((cw.shape[0], TN), lambda i, j: (0, j)),
        ],
        out_specs=pl.BlockSpec((TM, TN), lambda i, j: (i, j)),
        out_shape=jax.ShapeDtypeStruct((m, n), BF16),
        scratch_shapes=[
            pltpu.VMEM((TM, d), BF16),
            pltpu.VMEM((n // TN, SUBLANES, TN), F32),
            pltpu.VMEM((TM + SUBLANES, TN), F32),
        ],
        compiler_params=_compiler_params(("arbitrary", "arbitrary"), 40 * 1024 * 1024),
        name="ffn_up",
    )(x, gain, wu, wg, cw)


def _gdn_proj_kernel(x_ref, g_ref, w_ref, cw_ref, o_ref, h_ref, tail_ref, ext_ref,
                     *, tiles_per_seq, n_head_tiles, n_conv_tiles, head_dim):
    i = pl.program_id(0)
    j = pl.program_id(1)

    @pl.when(j == 0)
    def _():
        h_ref[...] = _rms_rows(x_ref[...], g_ref[...]).astype(BF16)

    y = _dot(h_ref[...], w_ref[...])

    @pl.when(j < n_conv_tiles)
    def _():
        first = (i % tiles_per_seq) == 0
        c = _silu(_causal_conv_tile(y, cw_ref, tail_ref, ext_ref, j, first))

        @pl.when(j < 2 * n_head_tiles)
        def _():
            scale = jnp.where(j < n_head_tiles, head_dim ** -0.5, 1.0)
            for hh in range(c.shape[1] // head_dim):
                sl = slice(hh * head_dim, (hh + 1) * head_dim)
                ch = c[:, sl]
                r = lax.rsqrt(jnp.sum(ch * ch, axis=-1, keepdims=True) + EPS)
                o_ref[:, sl] = (ch * r * scale).astype(o_ref.dtype)

        @pl.when(j >= 2 * n_head_tiles)
        def _():
            o_ref[...] = c.astype(o_ref.dtype)

    @pl.when(j >= n_conv_tiles)
    def _():
        o_ref[...] = y.astype(o_ref.dtype)


def _gdn_proj(x, gain, w, cw, seq, key_dim, head_dim):
    m, d = x.shape
    n = w.shape[1]
    n_conv_tiles = cw.shape[1] // TN
    kern = functools.partial(
        _gdn_proj_kernel, tiles_per_seq=seq // TM, n_head_tiles=key_dim // TN,
        n_conv_tiles=n_conv_tiles, head_dim=head_dim)
    return pl.pallas_call(
        kern,
        grid=(m // TM, n // TN),
        in_specs=[
            pl.BlockSpec((TM, d), lambda i, j: (i, 0)),
            pl.BlockSpec((1, d), lambda i, j: (0, 0)),
            pl.BlockSpec((d, TN), lambda i, j: (0, j)),
            pl.BlockSpec((cw.shape[0], TN), lambda i, j: (0, jnp.minimum(j, n_conv_tiles - 1))),
        ],
        out_specs=pl.BlockSpec((TM, TN), lambda i, j: (i, j)),
        out_shape=jax.ShapeDtypeStruct((m, n), BF16),
        scratch_shapes=[
            pltpu.VMEM((TM, d), BF16),
            pltpu.VMEM((n_conv_tiles, SUBLANES, TN), F32),
            pltpu.VMEM((TM + SUBLANES, TN), F32),
        ],
        compiler_params=_compiler_params(("arbitrary", "arbitrary"), 40 * 1024 * 1024),
        name="gdn_proj",
    )(x, gain, w, cw)


def _gdn_gate_kernel(x_ref, g_ref, w_ref, alog_ref, dtb_ref, o_ref, *, n_heads):
    h = _rms_rows(x_ref[...], g_ref[...]).astype(BF16)
    y = _dot(h, w_ref[...])
    is_decay = lax.broadcasted_iota(jnp.int32, (1, y.shape[1]), 1) < n_heads
    g = -jnp.exp(alog_ref[...]) * jax.nn.softplus(y + dtb_ref[...])
    vals = jnp.where(is_decay, g, jax.nn.sigmoid(y))
    ri = lax.broadcasted_iota(jnp.int32, (CHUNK, CHUNK), 0)
    ci = lax.broadcasted_iota(jnp.int32, (CHUNK, CHUNK), 1)
    tri = (ri >= ci).astype(F32)
    for c in range(y.shape[0] // CHUNK):
        rows = slice(c * CHUNK, (c + 1) * CHUNK)
        v = vals[rows, :]
        cum = jnp.dot(tri, v, precision=lax.Precision.HIGHEST, preferred_element_type=F32)
        o_ref[rows, :] = jnp.where(is_decay, cum, v)


def _gdn_gates(x, gain, w_ab, alog, dtb, n_heads):
    m, d = x.shape
    kern = functools.partial(_gdn_gate_kernel, n_heads=n_heads)
    return pl.pallas_call(
        kern,
        grid=(m // TM,),
        in_specs=[
            pl.BlockSpec((TM, d), lambda i: (i, 0)),
            pl.BlockSpec((1, d), lambda i: (0, 0)),
            pl.BlockSpec((d, LANES), lambda i: (0, 0)),
            pl.BlockSpec((1, LANES), lambda i: (0, 0)),
            pl.BlockSpec((1, LANES), lambda i: (0, 0)),
        ],
        out_specs=pl.BlockSpec((TM, LANES), lambda i: (i, 0)),
        out_shape=jax.ShapeDtypeStruct((m, LANES), F32),
        compiler_params=_compiler_params(("arbitrary",), 32 * 1024 * 1024),
        name="gdn_gates",
    )(x, gain, w_ab, alog, dtb)


def _mm_resid_kernel(a_ref, w_ref, r_ref, o_ref):
    o_ref[...] = r_ref[...] + _dot(a_ref[...], w_ref[...])


def _mm_resid(a, w, resid):
    m, k = a.shape
    n = w.shape[1]
    return pl.pallas_call(
        _mm_resid_kernel,
        grid=(m // TM, n // TN),
        in_specs=[
            pl.BlockSpec((TM, k), lambda i, j: (i, 0)),
            pl.BlockSpec((k, TN), lambda i, j: (0, j)),
            pl.BlockSpec((TM, TN), lambda i, j: (i, j)),
        ],
        out_specs=pl.BlockSpec((TM, TN), lambda i, j: (i, j)),
        out_shape=jax.ShapeDtypeStruct((m, n), F32),
        compiler_params=_compiler_params(("arbitrary", "arbitrary"), 48 * 1024 * 1024),
        name="mm_resid",
    )(a, w, resid)


def _bias_table_kernel(rb_ref, idx_ref, add_ref, o_ref):
    nrel = rb_ref.shape[1]
    r = lax.broadcasted_iota(jnp.int32, (nrel, idx_ref.shape[1]), 0)
    onehot = (r == idx_ref[...]).astype(F32)
    o_ref[...] = jnp.dot(rb_ref[...], onehot, precision=lax.Precision.HIGHEST,
                         preferred_element_type=F32) + add_ref[...]


def _band_layout(rows, keys):
    qpos = np.arange(rows)[:, None]
    kpos = np.arange(keys)[None, :] - (keys - rows)
    qc = qpos // CHUNK
    kc = np.floor_divide(kpos, CHUNK)
    in_band = (kc <= qc) & (kc >= qc - LEFT_CHUNKS)
    idx = np.clip(qpos - kpos, -(CHUNK - 1), MAX_REL) + (CHUNK - 1)
    idx = np.where(in_band, idx, 0).astype(np.int32)
    add = np.where(in_band, 0.0, MASK_VALUE).astype(np.float32)
    return idx.reshape(1, -1), add.reshape(1, -1)


def _bias_table(rel_bias, rows, keys):
    heads, nrel = rel_bias.shape
    nrel_pad = -(-nrel // LANES) * LANES
    rb = jnp.pad(rel_bias.astype(F32), ((0, 0), (0, nrel_pad - nrel)))
    idx, add = _band_layout(rows, keys)
    n = rows * keys
    tb = 2048
    out = pl.pallas_call(
        _bias_table_kernel,
        grid=(n // tb,),
        in_specs=[
            pl.BlockSpec((heads, nrel_pad), lambda i: (0, 0)),
            pl.BlockSpec((1, tb), lambda i: (0, i)),
            pl.BlockSpec((1, tb), lambda i: (0, i)),
        ],
        out_specs=pl.BlockSpec((heads, tb), lambda i: (0, i)),
        out_shape=jax.ShapeDtypeStruct((heads, n), F32),
        compiler_params=_compiler_params(("arbitrary",), 32 * 1024 * 1024),
        name="att_bias_table",
    )(rb, jnp.asarray(idx), jnp.asarray(add))
    return out.reshape(heads, rows, keys)


def _attn_kernel(q_ref, k0_ref, k1_ref, k2_ref, v0_ref, v1_ref, v2_ref, bias_ref, o_ref,
                 *, head_dim):
    qb = pl.program_id(2)
    rows = q_ref.shape[0]
    k_refs = (k0_ref, k1_ref, k2_ref)
    v_refs = (v0_ref, v1_ref, v2_ref)
    nblk = len(k_refs)
    for hh in range(q_ref.shape[1] // head_dim):
        sl = slice(hh * head_dim, (hh + 1) * head_dim)
        q = q_ref[:, sl]
        scores = []
        for t in range(nblk):
            s = _dot_nt(q, k_refs[t][:, sl]) + bias_ref[hh, :, t * rows:(t + 1) * rows]
            if t < nblk - 1:
                s = jnp.where(qb >= nblk - 1 - t, s, MASK_VALUE)
            scores.append(s)
        mx = scores[0].max(axis=-1, keepdims=True)
        for s in scores[1:]:
            mx = jnp.maximum(mx, s.max(axis=-1, keepdims=True))
        den = jnp.zeros_like(mx)
        acc = jnp.zeros((rows, head_dim), F32)
        for t in range(nblk):
            p = jnp.exp(scores[t] - mx)
            den = den + p.sum(axis=-1, keepdims=True)
            acc = acc + _dot(p.astype(BF16), v_refs[t][:, sl])
        o_ref[:, sl] = (acc / den).astype(o_ref.dtype)


def _band_attention(qkv, bias, batch, seq, d, head_dim):
    m = qkv.shape[0]
    r = ATT_ROWS
    gw = ATT_HEAD_GROUP * head_dim
    n_groups = d // gw
    nqb = seq // r
    nblk = bias.shape[2] // r

    def q_map(g, b, qb):
        return (b * nqb + qb, g)

    def kv_map(off, col0):
        def f(g, b, qb):
            return (b * nqb + jnp.maximum(qb - off, 0), col0 + g)
        return f

    k_specs = [pl.BlockSpec((r, gw), kv_map(nblk - 1 - t, n_groups)) for t in range(nblk)]
    v_specs = [pl.BlockSpec((r, gw), kv_map(nblk - 1 - t, 2 * n_groups)) for t in range(nblk)]
    kern = functools.partial(_attn_kernel, head_dim=head_dim)
    return pl.pallas_call(
        kern,
        grid=(n_groups, batch, nqb),
        in_specs=[pl.BlockSpec((r, gw), q_map)] + k_specs + v_specs + [
            pl.BlockSpec((ATT_HEAD_GROUP, r, nblk * r), lambda g, b, qb: (g, 0, 0))],
        out_specs=pl.BlockSpec((r, gw), q_map),
        out_shape=jax.ShapeDtypeStruct((m, d), BF16),
        compiler_params=_compiler_params(("arbitrary", "arbitrary", "arbitrary"),
                                         32 * 1024 * 1024),
        name="band_attention",
    )(qkv, qkv, qkv, qkv, qkv, qkv, qkv, bias)


def _pool_kernel(x_ref, g_ref, w_ref, sc_ref, o_ref, ext_ref, *, tiles_per_seq, windows):
    i = pl.program_id(0)
    tm, d = x_ref.shape
    halo = max(windows)
    x = x_ref[...]
    h = _rms_rows(x, g_ref[...])
    first = (i % tiles_per_seq) == 0

    @pl.when(first)
    def _():
        ext_ref[0:halo, :] = jnp.zeros((halo, d), F32)

    @pl.when(jnp.logical_not(first))
    def _():
        ext_ref[0:halo, :] = ext_ref[tm:tm + halo, :]

    ext_ref[halo:halo + tm, :] = h
    pos = (i % tiles_per_seq) * tm + lax.broadcasted_iota(jnp.int32, (tm, 1), 0)
    dg = d // len(windows)
    for g, w in enumerate(windows):
        cs = slice(g * dg, (g + 1) * dg)
        hg = h[:, cs]
        acc = hg
        for s in range(1, w):
            acc = acc + ext_ref[pl.ds(halo - s, tm), cs]
        cnt = jnp.minimum(pos + 1, w).astype(F32)
        pooled = acc / cnt - hg
        y = _dot(pooled.astype(BF16), w_ref[g]) * sc_ref[:, cs]
        o_ref[:, cs] = x[:, cs] + y


def _pool_mixer(x, gain, pool_w, pool_scale, seq):
    m, d = x.shape
    g, dg, _ = pool_w.shape
    halo = max(POOL_WINDOWS)
    kern = functools.partial(_pool_kernel, tiles_per_seq=seq // TM, windows=POOL_WINDOWS)
    return pl.pallas_call(
        kern,
        grid=(m // TM,),
        in_specs=[
            pl.BlockSpec((TM, d), lambda i: (i, 0)),
            pl.BlockSpec((1, d), lambda i: (0, 0)),
            pl.BlockSpec((g, dg, dg), lambda i: (0, 0, 0)),
            pl.BlockSpec((1, d), lambda i: (0, 0)),
        ],
        out_specs=pl.BlockSpec((TM, d), lambda i: (i, 0)),
        out_shape=jax.ShapeDtypeStruct((m, d), F32),
        scratch_shapes=[pltpu.VMEM((TM + halo, d), F32)],
        compiler_params=_compiler_params(("arbitrary",), 40 * 1024 * 1024),
        name="pool_mixer",
    )(x, gain, pool_w, pool_scale)


def _unit_lower_inverse(a, ri, ci, eye):
    def mm(p, q):
        return jnp.dot(p, q, precision=INV_PRECISION, preferred_element_type=F32)

    base = SUBLANES
    a0 = jnp.where((ri // base) == (ci // base), a, 0.0)
    p = mm(a0, a0)
    x = eye - a0
    x = x + mm(x, p)
    p = mm(p, p)
    x = x + mm(x, p)
    s = base
    while s < CHUNK:
        nmask = ((ri // (2 * s)) == (ci // (2 * s))) & ((ri // s) != (ci // s))
        n = jnp.where(nmask, a, 0.0)
        x = x - mm(x, mm(n, x))
        s *= 2
    return x


def _gdn_kernel(q_ref, k_ref, v_ref, gate_ref, gb_ref, rows_ref, og_ref, o_ref, s_ref,
                *, n_chunks, n_heads, dv):
    hk = pl.program_id(1)

    @pl.when(pl.program_id(2) == 0)
    def _():
        s_ref[...] = jnp.zeros(s_ref.shape, F32)

    c2 = 2 * CHUNK
    ri = lax.broadcasted_iota(jnp.int32, (c2, c2), 0)
    ci = lax.broadcasted_iota(jnp.int32, (c2, c2), 1)
    same = (ri // CHUNK) == (ci // CHUNK)
    causal = same & (ri >= ci)
    strict = same & (ri > ci)
    eye = (ri == ci).astype(F32)
    lane = lax.broadcasted_iota(jnp.int32, (CHUNK, LANES), 1)
    top = lax.broadcasted_iota(jnp.int32, (c2, 1), 0) < CHUNK
    og = og_ref[...]

    def chunk_body(c, carry):
        r0 = pl.multiple_of(c * CHUNK, CHUNK)
        q = q_ref[pl.ds(r0, CHUNK), :]
        k = k_ref[pl.ds(r0, CHUNK), :]
        v = v_ref[pl.ds(r0, CHUNK), :]
        gate = gate_ref[pl.ds(r0, CHUNK), :]
        gb = gb_ref[pl.ds(r0, CHUNK), :]
        rows = rows_ref[0, c]
        grow, brow, glast = rows[0:1], rows[1:2], rows[2:3]
        dec = jnp.exp(jnp.concatenate([rows[3:4], rows[4:5]], axis=1))

        def col(idx):
            return jnp.sum(jnp.where(lane == idx, gb, 0.0), axis=1, keepdims=True)

        gcol = jnp.concatenate([col(2 * hk), col(2 * hk + 1)], axis=0)
        bcol = jnp.concatenate([col(n_heads + 2 * hk), col(n_heads + 2 * hk + 1)], axis=0)

        k2 = jnp.concatenate([k, k], axis=0)
        q2 = jnp.concatenate([q, q], axis=0)
        kk = _dot_nt(k2, k2)
        qk = _dot_nt(q2, k2)
        diff = gcol - grow
        decay = jnp.where(causal, jnp.exp(jnp.where(causal, diff, 0.0)), 0.0)
        a = jnp.where(strict, bcol * kk * decay, 0.0)
        attn = qk * decay
        t = _unit_lower_inverse(a, ri, ci, eye)

        v2 = jnp.concatenate([v[:, :dv], v[:, dv:]], axis=0)
        u = _dot((t * brow).astype(BF16), v2)
        w = _dot((t * (brow * jnp.exp(grow))).astype(BF16), k2)

        s = s_ref[...]
        m1 = _dot(jnp.concatenate([w.astype(BF16), q2], axis=0), s.astype(BF16))
        ws = jnp.where(top, m1[:c2, :dv], m1[:c2, dv:])
        qs = jnp.where(top, m1[c2:, :dv], m1[c2:, dv:])
        vnew = u - ws
        vnb = vnew.astype(BF16)
        o = qs * jnp.exp(gcol) + _dot(attn.astype(BF16), vnb)

        kt = k2.astype(F32).T
        kte = (kt * jnp.exp(glast - grow)).astype(BF16)
        zeros = jnp.zeros((CHUNK, dv), BF16)
        vbd = jnp.concatenate([
            jnp.concatenate([vnb[:CHUNK], zeros], axis=1),
            jnp.concatenate([zeros, vnb[CHUNK:]], axis=1)], axis=0)
        s_ref[...] = s * dec + _dot(kte, vbd)

        gate2 = jnp.concatenate([gate[:, :dv], gate[:, dv:]], axis=0).astype(F32)
        on = o * lax.rsqrt(jnp.mean(o * o, axis=-1, keepdims=True) + EPS) * og
        out = on * _silu(gate2)
        o_ref[pl.ds(r0, CHUNK), :] = jnp.concatenate(
            [out[:CHUNK], out[CHUNK:]], axis=1).astype(o_ref.dtype)
        return carry

    lax.fori_loop(0, n_chunks, chunk_body, 0)


def _gdn_rows(gb, n_heads, n_kheads):
    m = gb.shape[0]
    nc = m // CHUNK
    rep = n_heads // n_kheads
    assert rep * CHUNK == LANES

    def stack(x):
        x = x.reshape(nc, CHUNK, n_kheads, rep).transpose(2, 0, 3, 1)
        return x.reshape(n_kheads, nc, rep * CHUNK)

    gc = gb[:, :n_heads].reshape(nc, CHUNK, n_heads)
    beta = gb[:, n_heads:2 * n_heads].reshape(nc, CHUNK, n_heads)
    last = jnp.broadcast_to(gc[:, CHUNK - 1:, :], gc.shape)
    last_k = gc[:, CHUNK - 1, :].reshape(nc, n_kheads, rep).transpose(1, 0, 2)
    per_head = [jnp.broadcast_to(last_k[:, :, r:r + 1], (n_kheads, nc, LANES)) for r in range(rep)]
    rows = [stack(gc), stack(beta), stack(last)] + per_head
    rows += [jnp.zeros_like(rows[0])] * (SUBLANES - len(rows))
    return jnp.stack(rows, axis=2)


def _gated_delta(proj, gb, rows, o_gain, batch, seq, n_kheads, n_heads, dk, dv):
    m = proj.shape[0]
    cb = GDN_CHUNKS_PER_STEP
    rb = cb * CHUNK
    ncb = seq // rb
    key_dim = n_kheads * dk
    val_dim = n_heads * dv
    vw = 2 * dv
    assert n_heads == 2 * n_kheads and dk == LANES and dv == LANES

    def rmap(b, hk, c):
        return b * ncb + c

    kern = functools.partial(_gdn_kernel, n_chunks=cb, n_heads=n_heads, dv=dv)
    return pl.pallas_call(
        kern,
        grid=(batch, n_kheads, ncb),
        in_specs=[
            pl.BlockSpec((rb, dk), lambda b, hk, c: (rmap(b, hk, c), hk)),
            pl.BlockSpec((rb, dk), lambda b, hk, c: (rmap(b, hk, c), key_dim // dk + hk)),
            pl.BlockSpec((rb, vw), lambda b, hk, c: (rmap(b, hk, c), 2 * key_dim // vw + hk)),
            pl.BlockSpec((rb, vw), lambda b, hk, c: (rmap(b, hk, c),
                                                     (2 * key_dim + val_dim) // vw + hk)),
            pl.BlockSpec((rb, LANES), lambda b, hk, c: (rmap(b, hk, c), 0)),
            pl.BlockSpec((1, cb, SUBLANES, LANES), lambda b, hk, c: (hk, rmap(b, hk, c), 0, 0)),
            pl.BlockSpec((1, dv), lambda b, hk, c: (0, 0)),
        ],
        out_specs=pl.BlockSpec((rb, vw), lambda b, hk, c: (rmap(b, hk, c), hk)),
        out_shape=jax.ShapeDtypeStruct((m, val_dim), BF16),
        scratch_shapes=[pltpu.VMEM((dk, vw), F32)],
        compiler_params=_compiler_params(("arbitrary", "arbitrary", "arbitrary"),
                                         32 * 1024 * 1024),
        name="gated_delta",
    )(proj, proj, proj, proj, gb, rows, o_gain)


def _pad_cols(w, n):
    return jnp.pad(w, ((0, 0), (0, n - w.shape[1])))


def _attention_layer(x, gain, w_qkv, q_gain, k_gain, rel_bias, w_o, batch, seq):
    d = x.shape[1]
    head_dim = d // ATT_HEADS
    qk_gain = jnp.zeros((SUBLANES, head_dim), F32)
    qk_gain = qk_gain.at[0].set(q_gain.astype(F32) * head_dim ** -0.5).at[1].set(k_gain.astype(F32))
    qkv = _qkv_proj(x, gain, w_qkv.astype(BF16), qk_gain, head_dim)
    keys = ATT_ROWS + LEFT_CHUNKS * CHUNK
    bias = _bias_table(rel_bias, ATT_ROWS, keys)
    o = _band_attention(qkv, bias, batch, seq, d, head_dim)
    return _mm_resid(o, w_o.astype(BF16), x)


def _gdn_layer(x, gain, w_in, conv_w, a_log, dt_bias, o_gain, w_o, batch, seq):
    d = x.shape[1]
    dk = d // GDN_K_HEADS
    dv = dk
    key_dim = GDN_K_HEADS * dk
    val_dim = GDN_V_HEADS * dv
    conv_ch = 2 * key_dim + val_dim
    n_main = conv_ch + val_dim
    proj = _gdn_proj(x, gain, w_in[:, :n_main].astype(BF16), conv_w.astype(F32), seq, key_dim, dk)
    w_ab = _pad_cols(w_in[:, n_main:], LANES).astype(BF16)
    pad = LANES - GDN_V_HEADS
    alog = jnp.pad(a_log.astype(F32), (0, pad)).reshape(1, LANES)
    dtb = jnp.pad(dt_bias.astype(F32), (0, pad)).reshape(1, LANES)
    gb = _gdn_gates(x, gain, w_ab, alog, dtb, GDN_V_HEADS)
    rows = _gdn_rows(gb, GDN_V_HEADS, GDN_K_HEADS)
    o = _gated_delta(proj, gb, rows, o_gain.astype(F32).reshape(1, dv), batch, seq,
                     GDN_K_HEADS, GDN_V_HEADS, dk, dv)
    return _mm_resid(o, w_o.astype(BF16), x)


def _conv_ffn(x, gain, w_up, conv_w, w_down, seq):
    d_ff = conv_w.shape[1]
    d_ff_pad = -(-d_ff // TN) * TN
    wu = _pad_cols(w_up[:, :d_ff], d_ff_pad).astype(BF16)
    wg = _pad_cols(w_up[:, d_ff:], d_ff_pad).astype(BF16)
    cw = _pad_cols(conv_w.astype(F32), d_ff_pad)
    wd = jnp.pad(w_down, ((0, d_ff_pad - d_ff), (0, 0))).astype(BF16)
    act = _ffn_up(x, gain, wu, wg, cw, seq)
    return _mm_resid(act, wd, x)


def kernel(x, mix_norm, ffn_norm, att_w_qkv, att_q_gain, att_k_gain, att_rel_bias, att_w_o,
           pool_w, pool_scale, gdn_w_in, gdn_conv, gdn_a_log, gdn_dt_bias, gdn_o_gain,
           gdn_w_o, ffn_w_up, ffn_conv, ffn_w_down):
    batch, seq, d = x.shape
    depth = mix_norm.shape[0]
    assert seq % TM == 0 and seq % (GDN_CHUNKS_PER_STEP * CHUNK) == 0 and seq % ATT_ROWS == 0
    xf = x.reshape(batch * seq, d).astype(F32)
    for i in range(depth):
        kind = i % N_MIXERS
        j = i // N_MIXERS
        gain = mix_norm[i].astype(F32).reshape(1, d)
        if kind == 0:
            xf = _attention_layer(xf, gain, att_w_qkv[j], att_q_gain[j], att_k_gain[j],
                                  att_rel_bias[j], att_w_o[j], batch, seq)
        elif kind == 1:
            xf = _pool_mixer(xf, gain, pool_w[j].astype(BF16),
                             pool_scale[j].astype(F32).reshape(1, d), seq)
        else:
            xf = _gdn_layer(xf, gain, gdn_w_in[j], gdn_conv[j], gdn_a_log[j], gdn_dt_bias[j],
                            gdn_o_gain[j], gdn_w_o[j], batch, seq)
        fgain = ffn_norm[i].astype(F32).reshape(1, d)
        xf = _conv_ffn(xf, fgain, ffn_w_up[i], ffn_conv[i], ffn_w_down[i], seq)
    return xf.reshape(batch, seq, d).astype(x.dtype)
```

```python
import functools

import numpy as np
import jax
import jax.numpy as jnp
from jax import lax
from jax.experimental import pallas as pl
from jax.experimental.pallas import tpu as pltpu

F32 = jnp.float32
BF16 = jnp.bfloat16

CHUNK = 64
EPS = 1e-6
MASK_VALUE = -1e30
N_MIXERS = 3

ATT_HEADS = 16
LEFT_CHUNKS = 8
MAX_REL = 256

POOL_WINDOWS = (2, 4, 8, 16)

GDN_K_HEADS = 16
GDN_V_HEADS = 32

LANES = 128
SUBLANES = 8
V7X_VMEM_BYTES = 64 * 1024 * 1024

TM = 512
TN = 512
ATT_ROWS = 256
ATT_HEAD_GROUP = 4
GDN_CHUNKS_PER_STEP = 8
GDN_PREP_UNROLL = 8


def _compiler_params(semantics, vmem_bytes):
    assert vmem_bytes < V7X_VMEM_BYTES
    return pltpu.CompilerParams(dimension_semantics=semantics, vmem_limit_bytes=vmem_bytes)


def _rms_rows(x, gain):
    ms = jnp.mean(x * x, axis=-1, keepdims=True)
    return x * lax.rsqrt(ms + EPS) * gain


def _silu(x):
    return x * jax.nn.sigmoid(x)


def _dot(a, b):
    return jnp.dot(a, b, preferred_element_type=F32)


def _dot_nt(a, b):
    return lax.dot_general(a, b, (((1,), (1,)), ((), ())), preferred_element_type=F32)


def _causal_conv_tile(u, cw_ref, tail_ref, ext_ref, j, first):
    tm = u.shape[0]
    k = cw_ref.shape[0]
    prev = jnp.where(first, 0.0, tail_ref[j])
    ext_ref[0:SUBLANES, :] = prev
    ext_ref[SUBLANES:SUBLANES + tm, :] = u
    tail_ref[j] = u[tm - SUBLANES:, :]
    acc = u * cw_ref[k - 1:k, :]
    for s in range(1, k):
        acc = acc + ext_ref[pl.ds(SUBLANES - s, tm), :] * cw_ref[k - 1 - s:k - s, :]
    return acc


def _qkv_kernel(x_ref, g_ref, w_ref, qkg_ref, o_ref, h_ref, *, n_q_tiles, head_dim):
    j = pl.program_id(1)

    @pl.when(j == 0)
    def _():
        h_ref[...] = _rms_rows(x_ref[...], g_ref[...]).astype(BF16)

    y = _dot(h_ref[...], w_ref[...])
    kind = j // n_q_tiles

    @pl.when(kind < 2)
    def _():
        gain = qkg_ref[pl.ds(kind, 1), :]
        for hh in range(y.shape[1] // head_dim):
            sl = slice(hh * head_dim, (hh + 1) * head_dim)
            yh = y[:, sl]
            r = lax.rsqrt(jnp.mean(yh * yh, axis=-1, keepdims=True) + EPS)
            o_ref[:, sl] = (yh * r * gain).astype(o_ref.dtype)

    @pl.when(kind == 2)
    def _():
        o_ref[...] = y.astype(o_ref.dtype)


def _qkv_proj(x, gain, w, qk_gain, head_dim):
    m, d = x.shape
    n = w.shape[1]
    kern = functools.partial(_qkv_kernel, n_q_tiles=d // TN, head_dim=head_dim)
    return pl.pallas_call(
        kern,
        grid=(m // TM, n // TN),
        in_specs=[
            pl.BlockSpec((TM, d), lambda i, j: (i, 0)),
            pl.BlockSpec((1, d), lambda i, j: (0, 0)),
            pl.BlockSpec((d, TN), lambda i, j: (0, j)),
            pl.BlockSpec((SUBLANES, head_dim), lambda i, j: (0, 0)),
        ],
        out_specs=pl.BlockSpec((TM, TN), lambda i, j: (i, j)),
        out_shape=jax.ShapeDtypeStruct((m, n), BF16),
        scratch_shapes=[pltpu.VMEM((TM, d), BF16)],
        compiler_params=_compiler_params(("arbitrary", "arbitrary"), 40 * 1024 * 1024),
        name="qkv_proj",
    )(x, gain, w, qk_gain)


def _ffn_up_kernel(x_ref, g_ref, wu_ref, wg_ref, cw_ref, o_ref, h_ref, tail_ref, ext_ref,
                   *, tiles_per_seq):
    i = pl.program_id(0)
    j = pl.program_id(1)

    @pl.when(j == 0)
    def _():
        h_ref[...] = _rms_rows(x_ref[...], g_ref[...]).astype(BF16)

    h = h_ref[...]
    u = _dot(h, wu_ref[...])
    gate = _dot(h, wg_ref[...])
    first = (i % tiles_per_seq) == 0
    c = _causal_conv_tile(u, cw_ref, tail_ref, ext_ref, j, first)
    o_ref[...] = (_silu(c) * gate).astype(o_ref.dtype)


def _ffn_up(x, gain, wu, wg, cw, seq):
    m, d = x.shape
    n = wu.shape[1]
    kern = functools.partial(_ffn_up_kernel, tiles_per_seq=seq // TM)
    return pl.pallas_call(
        kern,
        grid=(m // TM, n // TN),
        in_specs=[
            pl.BlockSpec((TM, d), lambda i, j: (i, 0)),
            pl.BlockSpec((1, d), lambda i, j: (0, 0)),
            pl.BlockSpec((d, TN), lambda i, j: (0, j)),
            pl.BlockSpec((d, TN), lambda i, j: (0, j)),
            pl.BlockSpec((cw.shape[0], TN), lambda i, j: (0, j)),
        ],
        out_specs=pl.BlockSpec((TM, TN), lambda i, j: (i, j)),
        out_shape=jax.ShapeDtypeStruct((m, n), BF16),
        scratch_shapes=[
            pltpu.VMEM((TM, d), BF16),
            pltpu.VMEM((n // TN, SUBLANES, TN), F32),
            pltpu.VMEM((TM + SUBLANES, TN), F32),
        ],
        compiler_params=_compiler_params(("arbitrary", "arbitrary"), 40 * 1024 * 1024),
        name="ffn_up",
    )(x, gain, wu, wg, cw)


def _gdn_proj_kernel(x_ref, g_ref, w_ref, cw_ref, o_ref, h_ref, tail_ref, ext_ref,
                     *, tiles_per_seq, n_head_tiles, n_conv_tiles, head_dim):
    i = pl.program_id(0)
    j = pl.program_id(1)

    @pl.when(j == 0)
    def _():
        h_ref[...] = _rms_rows(x_ref[...], g_ref[...]).astype(BF16)

    y = _dot(h_ref[...], w_ref[...])

    @pl.when(j < n_conv_tiles)
    def _():
        first = (i % tiles_per_seq) == 0
        c = _silu(_causal_conv_tile(y, cw_ref, tail_ref, ext_ref, j, first))

        @pl.when(j < 2 * n_head_tiles)
        def _():
            scale = jnp.where(j < n_head_tiles, head_dim ** -0.5, 1.0)
            for hh in range(c.shape[1] // head_dim):
                sl = slice(hh * head_dim, (hh + 1) * head_dim)
                ch = c[:, sl]
                r = lax.rsqrt(jnp.sum(ch * ch, axis=-1, keepdims=True) + EPS)
                o_ref[:, sl] = (ch * r * scale).astype(o_ref.dtype)

        @pl.when(j >= 2 * n_head_tiles)
        def _():
            o_ref[...] = c.astype(o_ref.dtype)

    @pl.when(j >= n_conv_tiles)
    def _():
        o_ref[...] = y.astype(o_ref.dtype)


def _gdn_proj(x, gain, w, cw, seq, key_dim, head_dim):
    m, d = x.shape
    n = w.shape[1]
    n_conv_tiles = cw.shape[1] // TN
    kern = functools.partial(
        _gdn_proj_kernel, tiles_per_seq=seq // TM, n_head_tiles=key_dim // TN,
        n_conv_tiles=n_conv_tiles, head_dim=head_dim)
    return pl.pallas_call(
        kern,
        grid=(m // TM, n // TN),
        in_specs=[
            pl.BlockSpec((TM, d), lambda i, j: (i, 0)),
            pl.BlockSpec((1, d), lambda i, j: (0, 0)),
            pl.BlockSpec((d, TN), lambda i, j: (0, j)),
            pl.BlockSpec((cw.shape[0], TN), lambda i, j: (0, jnp.minimum(j, n_conv_tiles - 1))),
        ],
        out_specs=pl.BlockSpec((TM, TN), lambda i, j: (i, j)),
        out_shape=jax.ShapeDtypeStruct((m, n), BF16),
        scratch_shapes=[
            pltpu.VMEM((TM, d), BF16),
            pltpu.VMEM((n_conv_tiles, SUBLANES, TN), F32),
            pltpu.VMEM((TM + SUBLANES, TN), F32),
        ],
        compiler_params=_compiler_params(("arbitrary", "arbitrary"), 40 * 1024 * 1024),
        name="gdn_proj",
    )(x, gain, w, cw)


def _gdn_gate_kernel(x_ref, g_ref, w_ref, alog_ref, dtb_ref, o_ref, *, n_heads):
    h = _rms_rows(x_ref[...], g_ref[...]).astype(BF16)
    y = _dot(h, w_ref[...])
    is_decay = lax.broadcasted_iota(jnp.int32, (1, y.shape[1]), 1) < n_heads
    g = -jnp.exp(alog_ref[...]) * jax.nn.softplus(y + dtb_ref[...])
    vals = jnp.where(is_decay, g, jax.nn.sigmoid(y))
    ri = lax.broadcasted_iota(jnp.int32, (CHUNK, CHUNK), 0)
    ci = lax.broadcasted_iota(jnp.int32, (CHUNK, CHUNK), 1)
    tri = (ri >= ci).astype(F32)
    for c in range(y.shape[0] // CHUNK):
        rows = slice(c * CHUNK, (c + 1) * CHUNK)
        v = vals[rows, :]
        cum = jnp.dot(tri, v, precision=lax.Precision.HIGHEST, preferred_element_type=F32)
        o_ref[rows, :] = jnp.where(is_decay, cum, v)


def _gdn_gates(x, gain, w_ab, alog, dtb, n_heads):
    m, d = x.shape
    kern = functools.partial(_gdn_gate_kernel, n_heads=n_heads)
    return pl.pallas_call(
        kern,
        grid=(m // TM,),
        in_specs=[
            pl.BlockSpec((TM, d), lambda i: (i, 0)),
            pl.BlockSpec((1, d), lambda i: (0, 0)),
            pl.BlockSpec((d, LANES), lambda i: (0, 0)),
            pl.BlockSpec((1, LANES), lambda i: (0, 0)),
            pl.BlockSpec((1, LANES), lambda i: (0, 0)),
        ],
        out_specs=pl.BlockSpec((TM, LANES), lambda i: (i, 0)),
        out_shape=jax.ShapeDtypeStruct((m, LANES), F32),
        compiler_params=_compiler_params(("arbitrary",), 32 * 1024 * 1024),
        name="gdn_gates",
    )(x, gain, w_ab, alog, dtb)


def _mm_resid_kernel(a_ref, w_ref, r_ref, o_ref):
    o_ref[...] = r_ref[...] + _dot(a_ref[...], w_ref[...])


def _mm_resid(a, w, resid):
    m, k = a.shape
    n = w.shape[1]
    return pl.pallas_call(
        _mm_resid_kernel,
        grid=(m // TM, n // TN),
        in_specs=[
            pl.BlockSpec((TM, k), lambda i, j: (i, 0)),
            pl.BlockSpec((k, TN), lambda i, j: (0, j)),
            pl.BlockSpec((TM, TN), lambda i, j: (i, j)),
        ],
        out_specs=pl.BlockSpec((TM, TN), lambda i, j: (i, j)),
        out_shape=jax.ShapeDtypeStruct((m, n), F32),
        compiler_params=_compiler_params(("arbitrary", "arbitrary"), 48 * 1024 * 1024),
        name="mm_resid",
    )(a, w, resid)


def _bias_table_kernel(rb_ref, idx_ref, add_ref, o_ref):
    nrel = rb_ref.shape[1]
    r = lax.broadcasted_iota(jnp.int32, (nrel, idx_ref.shape[1]), 0)
    onehot = (r == idx_ref[...]).astype(F32)
    o_ref[...] = jnp.dot(rb_ref[...], onehot, precision=lax.Precision.HIGHEST,
                         preferred_element_type=F32) + add_ref[...]


def _band_layout(rows, keys):
    qpos = np.arange(rows)[:, None]
    kpos = np.arange(keys)[None, :] - (keys - rows)
    qc = qpos // CHUNK
    kc = np.floor_divide(kpos, CHUNK)
    in_band = (kc <= qc) & (kc >= qc - LEFT_CHUNKS)
    idx = np.clip(qpos - kpos, -(CHUNK - 1), MAX_REL) + (CHUNK - 1)
    idx = np.where(in_band, idx, 0).astype(np.int32)
    add = np.where(in_band, 0.0, MASK_VALUE).astype(np.float32)
    return idx.reshape(1, -1), add.reshape(1, -1)


def _bias_table(rel_bias, rows, keys):
    heads, nrel = rel_bias.shape
    nrel_pad = -(-nrel // LANES) * LANES
    rb = jnp.pad(rel_bias.astype(F32), ((0, 0), (0, nrel_pad - nrel)))
    idx, add = _band_layout(rows, keys)
    n = rows * keys
    tb = 2048
    out = pl.pallas_call(
        _bias_table_kernel,
        grid=(n // tb,),
        in_specs=[
            pl.BlockSpec((heads, nrel_pad), lambda i: (0, 0)),
            pl.BlockSpec((1, tb), lambda i: (0, i)),
            pl.BlockSpec((1, tb), lambda i: (0, i)),
        ],
        out_specs=pl.BlockSpec((heads, tb), lambda i: (0, i)),
        out_shape=jax.ShapeDtypeStruct((heads, n), F32),
        compiler_params=_compiler_params(("arbitrary",), 32 * 1024 * 1024),
        name="att_bias_table",
    )(rb, jnp.asarray(idx), jnp.asarray(add))
    return out.reshape(heads, rows, keys)


def _attn_kernel(q_ref, k0_ref, k1_ref, k2_ref, v0_ref, v1_ref, v2_ref, bias_ref, o_ref,
                 *, head_dim):
    qb = pl.program_id(2)
    rows = q_ref.shape[0]
    k_refs = (k0_ref, k1_ref, k2_ref)
    v_refs = (v0_ref, v1_ref, v2_ref)
    nblk = len(k_refs)
    for hh in range(q_ref.shape[1] // head_dim):
        sl = slice(hh * head_dim, (hh + 1) * head_dim)
        q = q_ref[:, sl]
        scores = []
        for t in range(nblk):
            s = _dot_nt(q, k_refs[t][:, sl]) + bias_ref[hh, :, t * rows:(t + 1) * rows]
            if t < nblk - 1:
                s = jnp.where(qb >= nblk - 1 - t, s, MASK_VALUE)
            scores.append(s)
        mx = scores[0].max(axis=-1, keepdims=True)
        for s in scores[1:]:
            mx = jnp.maximum(mx, s.max(axis=-1, keepdims=True))
        den = jnp.zeros_like(mx)
        acc = jnp.zeros((rows, head_dim), F32)
        for t in range(nblk):
            p = jnp.exp(scores[t] - mx)
            den = den + p.sum(axis=-1, keepdims=True)
            acc = acc + _dot(p.astype(BF16), v_refs[t][:, sl])
        o_ref[:, sl] = (acc / den).astype(o_ref.dtype)


def _band_attention(qkv, bias, batch, seq, d, head_dim):
    m = qkv.shape[0]
    r = ATT_ROWS
    gw = ATT_HEAD_GROUP * head_dim
    n_groups = d // gw
    nqb = seq // r
    nblk = bias.shape[2] // r

    def q_map(g, b, qb):
        return (b * nqb + qb, g)

    def kv_map(off, col0):
        def f(g, b, qb):
            return (b * nqb + jnp.maximum(qb - off, 0), col0 + g)
        return f

    k_specs = [pl.BlockSpec((r, gw), kv_map(nblk - 1 - t, n_groups)) for t in range(nblk)]
    v_specs = [pl.BlockSpec((r, gw), kv_map(nblk - 1 - t, 2 * n_groups)) for t in range(nblk)]
    kern = functools.partial(_attn_kernel, head_dim=head_dim)
    return pl.pallas_call(
        kern,
        grid=(n_groups, batch, nqb),
        in_specs=[pl.BlockSpec((r, gw), q_map)] + k_specs + v_specs + [
            pl.BlockSpec((ATT_HEAD_GROUP, r, nblk * r), lambda g, b, qb: (g, 0, 0))],
        out_specs=pl.BlockSpec((r, gw), q_map),
        out_shape=jax.ShapeDtypeStruct((m, d), BF16),
        compiler_params=_compiler_params(("arbitrary", "arbitrary", "arbitrary"),
                                         32 * 1024 * 1024),
        name="band_attention",
    )(qkv, qkv, qkv, qkv, qkv, qkv, qkv, bias)


def _pool_kernel(x_ref, g_ref, w_ref, sc_ref, o_ref, ext_ref, *, tiles_per_seq, windows):
    i = pl.program_id(0)
    tm, d = x_ref.shape
    halo = max(windows)
    x = x_ref[...]
    h = _rms_rows(x, g_ref[...])
    first = (i % tiles_per_seq) == 0

    @pl.when(first)
    def _():
        ext_ref[0:halo, :] = jnp.zeros((halo, d), F32)

    @pl.when(jnp.logical_not(first))
    def _():
        ext_ref[0:halo, :] = ext_ref[tm:tm + halo, :]

    ext_ref[halo:halo + tm, :] = h
    pos = (i % tiles_per_seq) * tm + lax.broadcasted_iota(jnp.int32, (tm, 1), 0)
    dg = d // len(windows)
    for g, w in enumerate(windows):
        cs = slice(g * dg, (g + 1) * dg)
        hg = h[:, cs]
        acc = hg
        for s in range(1, w):
            acc = acc + ext_ref[pl.ds(halo - s, tm), cs]
        cnt = jnp.minimum(pos + 1, w).astype(F32)
        pooled = acc / cnt - hg
        y = _dot(pooled.astype(BF16), w_ref[g]) * sc_ref[:, cs]
        o_ref[:, cs] = x[:, cs] + y


def _pool_mixer(x, gain, pool_w, pool_scale, seq):
    m, d = x.shape
    g, dg, _ = pool_w.shape
    halo = max(POOL_WINDOWS)
    kern = functools.partial(_pool_kernel, tiles_per_seq=seq // TM, windows=POOL_WINDOWS)
    return pl.pallas_call(
        kern,
        grid=(m // TM,),
        in_specs=[
            pl.BlockSpec((TM, d), lambda i: (i, 0)),
            pl.BlockSpec((1, d), lambda i: (0, 0)),
            pl.BlockSpec((g, dg, dg), lambda i: (0, 0, 0)),
            pl.BlockSpec((1, d), lambda i: (0, 0)),
        ],
        out_specs=pl.BlockSpec((TM, d), lambda i: (i, 0)),
        out_shape=jax.ShapeDtypeStruct((m, d), F32),
        scratch_shapes=[pltpu.VMEM((TM + halo, d), F32)],
        compiler_params=_compiler_params(("arbitrary",), 40 * 1024 * 1024),
        name="pool_mixer",
    )(x, gain, pool_w, pool_scale)


def _mm_inv(p, q):
    return _dot(p.astype(BF16), q.astype(BF16))


def _unit_lower_inverse(mats, ri, ci, eye):
    base = SUBLANES
    bmask = (ri // base) == (ci // base)
    a0 = [jnp.where(bmask, a, 0.0) for a in mats]
    p = [_mm_inv(m, m) for m in a0]
    x = [eye - m for m in a0]
    x = [xi + _mm_inv(xi, pi) for xi, pi in zip(x, p)]
    p = [_mm_inv(pi, pi) for pi in p]
    x = [xi + _mm_inv(xi, pi) for xi, pi in zip(x, p)]
    s = base
    while s < CHUNK:
        nmask = ((ri // (2 * s)) == (ci // (2 * s))) & ((ri // s) != (ci // s))
        n = [jnp.where(nmask, a, 0.0) for a in mats]
        y = [_mm_inv(ni, xi) for ni, xi in zip(n, x)]
        x = [xi - _mm_inv(xi, yi) for xi, yi in zip(x, y)]
        s *= 2
    return x


def _gdn_kernel(q_ref, k_ref, v_ref, gate_ref, gb_ref, rows_ref, og_ref, o_ref,
                s_ref, gq_ref, ho_ref, *, n_chunks, n_heads, dv, unroll):
    hk = pl.program_id(1)

    @pl.when(pl.program_id(2) == 0)
    def _():
        s_ref[...] = jnp.zeros(s_ref.shape, F32)

    c2 = 2 * CHUNK
    ri = lax.broadcasted_iota(jnp.int32, (c2, c2), 0)
    ci = lax.broadcasted_iota(jnp.int32, (c2, c2), 1)
    same = (ri // CHUNK) == (ci // CHUNK)
    causal = same & (ri >= ci)
    strict = same & (ri > ci)
    eye = (ri == ci).astype(F32)
    head0_lanes = ci < CHUNK
    lane = lax.broadcasted_iota(jnp.int32, (CHUNK, LANES), 1)
    og = og_ref[...]

    def col(gb, idx):
        return jnp.sum(jnp.where(lane == idx, gb, 0.0), axis=1, keepdims=True)

    def prepare(cs):
        n = range(len(cs))
        r0 = [pl.multiple_of(c * CHUNK, CHUNK) for c in cs]
        q = [q_ref[pl.ds(r, CHUNK), :] for r in r0]
        k = [k_ref[pl.ds(r, CHUNK), :] for r in r0]
        v = [v_ref[pl.ds(r, CHUNK), :] for r in r0]
        gb = [gb_ref[pl.ds(r, CHUNK), :] for r in r0]
        rows = [rows_ref[0, c] for c in cs]
        grow = [x[0:1] for x in rows]
        brow = [x[1:2] for x in rows]
        glast = [x[2:3] for x in rows]
        gcol = [jnp.concatenate([col(g, 2 * hk), col(g, 2 * hk + 1)], axis=0) for g in gb]
        bcol = [jnp.concatenate([col(g, n_heads + 2 * hk), col(g, n_heads + 2 * hk + 1)], axis=0)
                for g in gb]

        k2 = [jnp.concatenate([x, x], axis=0) for x in k]
        qkk = [_dot_nt(jnp.concatenate([q[i], k[i]], axis=0), k2[i]) for i in n]
        qk = [jnp.concatenate([x[:CHUNK], x[:CHUNK]], axis=0) for x in qkk]
        kk = [jnp.concatenate([x[CHUNK:], x[CHUNK:]], axis=0) for x in qkk]
        decay = [jnp.where(causal, jnp.exp(jnp.where(causal, gcol[i] - grow[i], 0.0)), 0.0)
                 for i in n]
        a = [jnp.where(strict, bcol[i] * kk[i] * decay[i], 0.0) for i in n]
        attn = [qk[i] * decay[i] for i in n]
        t = _unit_lower_inverse(a, ri, ci, eye)

        v2 = [jnp.concatenate([x[:, :dv], x[:, dv:]], axis=0) for x in v]
        u = [_dot((t[i] * brow[i]).astype(BF16), v2[i]) for i in n]
        w = [_dot((t[i] * (brow[i] * jnp.exp(grow[i]))).astype(BF16), k2[i]) for i in n]
        wu = [jnp.concatenate([w[i], u[i]], axis=1).astype(BF16) for i in n]

        kte = [k2[i].astype(F32).T * jnp.exp(glast[i] - grow[i]) for i in n]
        lhs = [jnp.concatenate([
            jnp.where(head0_lanes, kte[i], 0.0),
            jnp.where(head0_lanes, 0.0, kte[i]),
            attn[i]], axis=0).astype(BF16) for i in n]
        r = [_dot(lhs[i], wu[i]) for i in n]
        for i in n:
            q2f = jnp.concatenate([q[i], q[i]], axis=0).astype(F32)
            qmat = q2f * jnp.exp(gcol[i]) - r[i][2 * c2:, :dv]
            omat = r[i][2 * c2:, dv:]
            gq_ref[cs[i]] = jnp.concatenate([
                -r[i][:c2, :dv], qmat[:CHUNK], -r[i][c2:2 * c2, :dv], qmat[CHUNK:]],
                axis=0).astype(BF16)
            ho_ref[cs[i]] = jnp.concatenate([
                r[i][:c2, dv:], omat[:CHUNK], r[i][c2:2 * c2, dv:], omat[CHUNK:]], axis=0)

    def prepare_group(cg, carry):
        prepare([cg * unroll + cc for cc in range(unroll)])
        return carry

    lax.fori_loop(0, n_chunks // unroll, prepare_group, 0)

    blk = c2 + CHUNK

    def scan(c, states):
        r0 = pl.multiple_of(c * CHUNK, CHUNK)
        rows = rows_ref[0, c]
        gate = gate_ref[pl.ds(r0, CHUNK), :]
        new_states = []
        outs = []
        for j in range(2):
            s = states[j]
            r = _dot(gq_ref[c, j * blk:(j + 1) * blk, :], s.astype(BF16))
            ho = ho_ref[c, j * blk:(j + 1) * blk, :]
            dec = jnp.exp(rows[3 + j:4 + j])
            new_states.append(s * dec + r[:c2] + ho[:c2])
            o = r[c2:] + ho[c2:]
            on = o * lax.rsqrt(jnp.mean(o * o, axis=-1, keepdims=True) + EPS) * og
            outs.append(on * _silu(gate[:, j * dv:(j + 1) * dv].astype(F32)))
        o_ref[pl.ds(r0, CHUNK), :] = jnp.concatenate(outs, axis=1).astype(o_ref.dtype)
        return tuple(new_states)

    s0, s1 = lax.fori_loop(0, n_chunks, scan, (s_ref[0], s_ref[1]))
    s_ref[0] = s0
    s_ref[1] = s1


def _gdn_rows(gb, n_heads, n_kheads):
    m = gb.shape[0]
    nc = m // CHUNK
    rep = n_heads // n_kheads
    assert rep * CHUNK == LANES

    def stack(x):
        x = x.reshape(nc, CHUNK, n_kheads, rep).transpose(2, 0, 3, 1)
        return x.reshape(n_kheads, nc, rep * CHUNK)

    gc = gb[:, :n_heads].reshape(nc, CHUNK, n_heads)
    beta = gb[:, n_heads:2 * n_heads].reshape(nc, CHUNK, n_heads)
    last = jnp.broadcast_to(gc[:, CHUNK - 1:, :], gc.shape)
    last_k = gc[:, CHUNK - 1, :].reshape(nc, n_kheads, rep).transpose(1, 0, 2)
    per_head = [jnp.broadcast_to(last_k[:, :, r:r + 1], (n_kheads, nc, LANES)) for r in range(rep)]
    rows = [stack(gc), stack(beta), stack(last)] + per_head
    rows += [jnp.zeros_like(rows[0])] * (SUBLANES - len(rows))
    return jnp.stack(rows, axis=2)


def _gated_delta(proj, gb, rows, o_gain, batch, seq, n_kheads, n_heads, dk, dv):
    m = proj.shape[0]
    cb = GDN_CHUNKS_PER_STEP
    rb = cb * CHUNK
    ncb = seq // rb
    key_dim = n_kheads * dk
    val_dim = n_heads * dv
    vw = 2 * dv
    assert n_heads == 2 * n_kheads and dk == LANES and dv == LANES

    def rmap(b, hk, c):
        return b * ncb + c

    kern = functools.partial(_gdn_kernel, n_chunks=cb, n_heads=n_heads, dv=dv,
                             unroll=GDN_PREP_UNROLL)
    return pl.pallas_call(
        kern,
        grid=(batch, n_kheads, ncb),
        in_specs=[
            pl.BlockSpec((rb, dk), lambda b, hk, c: (rmap(b, hk, c), hk)),
            pl.BlockSpec((rb, dk), lambda b, hk, c: (rmap(b, hk, c), key_dim // dk + hk)),
            pl.BlockSpec((rb, vw), lambda b, hk, c: (rmap(b, hk, c), 2 * key_dim // vw + hk)),
            pl.BlockSpec((rb, vw), lambda b, hk, c: (rmap(b, hk, c),
                                                     (2 * key_dim + val_dim) // vw + hk)),
            pl.BlockSpec((rb, LANES), lambda b, hk, c: (rmap(b, hk, c), 0)),
            pl.BlockSpec((1, cb, SUBLANES, LANES), lambda b, hk, c: (hk, rmap(b, hk, c), 0, 0)),
            pl.BlockSpec((1, dv), lambda b, hk, c: (0, 0)),
        ],
        out_specs=pl.BlockSpec((rb, vw), lambda b, hk, c: (rmap(b, hk, c), hk)),
        out_shape=jax.ShapeDtypeStruct((m, val_dim), BF16),
        scratch_shapes=[
            pltpu.VMEM((2, dk, dv), F32),
            pltpu.VMEM((cb, 2 * (dk + CHUNK), dv), BF16),
            pltpu.VMEM((cb, 2 * (dk + CHUNK), dv), F32),
        ],
        compiler_params=_compiler_params(("arbitrary", "arbitrary", "arbitrary"),
                                         32 * 1024 * 1024),
        name="gated_delta",
    )(proj, proj, proj, proj, gb, rows, o_gain)


def _pad_cols(w, n):
    return jnp.pad(w, ((0, 0), (0, n - w.shape[1])))


def _attention_layer(x, gain, w_qkv, q_gain, k_gain, rel_bias, w_o, batch, seq):
    d = x.shape[1]
    head_dim = d // ATT_HEADS
    qk_gain = jnp.zeros((SUBLANES, head_dim), F32)
    qk_gain = qk_gain.at[0].set(q_gain.astype(F32) * head_dim ** -0.5).at[1].set(k_gain.astype(F32))
    qkv = _qkv_proj(x, gain, w_qkv.astype(BF16), qk_gain, head_dim)
    keys = ATT_ROWS + LEFT_CHUNKS * CHUNK
    bias = _bias_table(rel_bias, ATT_ROWS, keys)
    o = _band_attention(qkv, bias, batch, seq, d, head_dim)
    return _mm_resid(o, w_o.astype(BF16), x)


def _gdn_layer(x, gain, w_in, conv_w, a_log, dt_bias, o_gain, w_o, batch, seq):
    d = x.shape[1]
    dk = d // GDN_K_HEADS
    dv = dk
    key_dim = GDN_K_HEADS * dk
    val_dim = GDN_V_HEADS * dv
    conv_ch = 2 * key_dim + val_dim
    n_main = conv_ch + val_dim
    proj = _gdn_proj(x, gain, w_in[:, :n_main].astype(BF16), conv_w.astype(F32), seq, key_dim, dk)
    w_ab = _pad_cols(w_in[:, n_main:], LANES).astype(BF16)
    pad = LANES - GDN_V_HEADS
    alog = jnp.pad(a_log.astype(F32), (0, pad)).reshape(1, LANES)
    dtb = jnp.pad(dt_bias.astype(F32), (0, pad)).reshape(1, LANES)
    gb = _gdn_gates(x, gain, w_ab, alog, dtb, GDN_V_HEADS)
    rows = _gdn_rows(gb, GDN_V_HEADS, GDN_K_HEADS)
    o = _gated_delta(proj, gb, rows, o_gain.astype(F32).reshape(1, dv), batch, seq,
                     GDN_K_HEADS, GDN_V_HEADS, dk, dv)
    return _mm_resid(o, w_o.astype(BF16), x)


def _conv_ffn(x, gain, w_up, conv_w, w_down, seq):
    d_ff = conv_w.shape[1]
    d_ff_pad = -(-d_ff // TN) * TN
    wu = _pad_cols(w_up[:, :d_ff], d_ff_pad).astype(BF16)
    wg = _pad_cols(w_up[:, d_ff:], d_ff_pad).astype(BF16)
    cw = _pad_cols(conv_w.astype(F32), d_ff_pad)
    wd = jnp.pad(w_down, ((0, d_ff_pad - d_ff), (0, 0))).astype(BF16)
    act = _ffn_up(x, gain, wu, wg, cw, seq)
    return _mm_resid(act, wd, x)


def kernel(x, mix_norm, ffn_norm, att_w_qkv, att_q_gain, att_k_gain, att_rel_bias, att_w_o,
           pool_w, pool_scale, gdn_w_in, gdn_conv, gdn_a_log, gdn_dt_bias, gdn_o_gain,
           gdn_w_o, ffn_w_up, ffn_conv, ffn_w_down):
    batch, seq, d = x.shape
    depth = mix_norm.shape[0]
    assert seq % TM == 0 and seq % (GDN_CHUNKS_PER_STEP * CHUNK) == 0 and seq % ATT_ROWS == 0
    xf = x.reshape(batch * seq, d).astype(F32)
    for i in range(depth):
        kind = i % N_MIXERS
        j = i // N_MIXERS
        gain = mix_norm[i].astype(F32).reshape(1, d)
        if kind == 0:
            xf = _attention_layer(xf, gain, att_w_qkv[j], att_q_gain[j], att_k_gain[j],
                                  att_rel_bias[j], att_w_o[j], batch, seq)
        elif kind == 1:
            xf = _pool_mixer(xf, gain, pool_w[j].astype(BF16),
                             pool_scale[j].astype(F32).reshape(1, d), seq)
        else:
            xf = _gdn_layer(xf, gain, gdn_w_in[j], gdn_conv[j], gdn_a_log[j], gdn_dt_bias[j],
                            gdn_o_gain[j], gdn_w_o[j], batch, seq)
        fgain = ffn_norm[i].astype(F32).reshape(1, d)
        xf = _conv_ffn(xf, fgain, ffn_w_up[i], ffn_conv[i], ffn_w_down[i], seq)
    return xf.reshape(batch, seq, d).astype(x.dtype)
```

```python
import functools

import numpy as np
import jax
import jax.numpy as jnp
from jax import lax
from jax.experimental import pallas as pl
from jax.experimental.pallas import tpu as pltpu

F32 = jnp.float32
BF16 = jnp.bfloat16

CHUNK = 64
EPS = 1e-6
MASK_VALUE = -1e30
N_MIXERS = 3

ATT_HEADS = 16
LEFT_CHUNKS = 8
MAX_REL = 256

POOL_WINDOWS = (2, 4, 8, 16)

GDN_K_HEADS = 16
GDN_V_HEADS = 32

LANES = 128
SUBLANES = 8
V7X_VMEM_BYTES = 64 * 1024 * 1024

TM = 1024
TN = 1024
SUB = 512
POOL_TM = 512
ATT_ROWS = 256
ATT_HEAD_GROUP = 4
GDN_CHUNKS_PER_STEP = 8
GDN_PREP_UNROLL = 8


def _compiler_params(semantics, vmem_bytes):
    assert vmem_bytes < V7X_VMEM_BYTES
    return pltpu.CompilerParams(dimension_semantics=semantics, vmem_limit_bytes=vmem_bytes)


def _rms_rows(x, gain):
    ms = jnp.mean(x * x, axis=-1, keepdims=True)
    return x * lax.rsqrt(ms + EPS) * gain


def _silu(x):
    return x * jax.nn.sigmoid(x)


def _dot(a, b):
    return jnp.dot(a, b, preferred_element_type=F32)


def _dot_nt(a, b):
    return lax.dot_general(a, b, (((1,), (1,)), ((), ())), preferred_element_type=F32)


def _causal_conv_tile(u, cw, tail_ref, ext_ref, t, first):
    tm = u.shape[0]
    k = cw.shape[0]
    prev = jnp.where(first, 0.0, tail_ref[t])
    ext_ref[0:SUBLANES, :] = prev
    ext_ref[SUBLANES:SUBLANES + tm, :] = u
    tail_ref[t] = u[tm - SUBLANES:, :]
    acc = u * cw[k - 1:k, :]
    for s in range(1, k):
        acc = acc + ext_ref[pl.ds(SUBLANES - s, tm), :] * cw[k - 1 - s:k - s, :]
    return acc


def _head_norm_store(y, o_ref, col0, head_dim, gain, mean):
    for hh in range(y.shape[1] // head_dim):
        sl = slice(hh * head_dim, (hh + 1) * head_dim)
        yh = y[:, sl]
        ss = jnp.sum(yh * yh, axis=-1, keepdims=True)
        r = lax.rsqrt((ss / head_dim if mean else ss) + EPS)
        o_ref[:, col0 + hh * head_dim:col0 + (hh + 1) * head_dim] = (yh * r * gain).astype(o_ref.dtype)


def _qkv_kernel(x_ref, g_ref, w_ref, qkg_ref, o_ref, h_ref, *, steps_per_kind, head_dim):
    j = pl.program_id(1)

    @pl.when(j == 0)
    def _():
        h_ref[...] = _rms_rows(x_ref[...], g_ref[...]).astype(BF16)

    kind = j // steps_per_kind
    pieces = range(o_ref.shape[1] // SUB)

    @pl.when(kind < 2)
    def _():
        gain = qkg_ref[pl.ds(kind, 1), :]
        for t in pieces:
            y = _dot(h_ref[...], w_ref[:, t * SUB:(t + 1) * SUB])
            _head_norm_store(y, o_ref, t * SUB, head_dim, gain, mean=True)

    @pl.when(kind == 2)
    def _():
        for t in pieces:
            cols = slice(t * SUB, (t + 1) * SUB)
            o_ref[:, cols] = _dot(h_ref[...], w_ref[:, cols]).astype(o_ref.dtype)


def _qkv_proj(x, gain, w, qk_gain, head_dim):
    m, d = x.shape
    n = w.shape[1]
    kern = functools.partial(_qkv_kernel, steps_per_kind=d // TN, head_dim=head_dim)
    return pl.pallas_call(
        kern,
        grid=(m // TM, n // TN),
        in_specs=[
            pl.BlockSpec((TM, d), lambda i, j: (i, 0)),
            pl.BlockSpec((1, d), lambda i, j: (0, 0)),
            pl.BlockSpec((d, TN), lambda i, j: (0, j)),
            pl.BlockSpec((SUBLANES, head_dim), lambda i, j: (0, 0)),
        ],
        out_specs=pl.BlockSpec((TM, TN), lambda i, j: (i, j)),
        out_shape=jax.ShapeDtypeStruct((m, n), BF16),
        scratch_shapes=[pltpu.VMEM((TM, d), BF16)],
        compiler_params=_compiler_params(("arbitrary", "arbitrary"), 48 * 1024 * 1024),
        name="qkv_proj",
    )(x, gain, w, qk_gain)


def _ffn_kernel(x_ref, g_ref, wu_ref, wg_ref, cw_ref, wd_ref, o_ref, h_ref, tail_ref, ext_ref,
                *, tiles_per_seq):
    i = pl.program_id(0)
    j = pl.program_id(1)

    @pl.when(j == 0)
    def _():
        x = x_ref[...]
        h_ref[...] = _rms_rows(x, g_ref[...]).astype(BF16)
        o_ref[...] = x

    h = h_ref[...]
    u = _dot(h, wu_ref[...])
    gate = _dot(h, wg_ref[...])
    first = (i % tiles_per_seq) == 0
    c = _causal_conv_tile(u, cw_ref[...], tail_ref, ext_ref, j, first)
    act = (_silu(c) * gate).astype(BF16)
    o_ref[...] += _dot(act, wd_ref[...])


def _conv_ffn_call(x, gain, wu, wg, cw, wd, seq):
    m, d = x.shape
    n = wu.shape[1]
    kern = functools.partial(_ffn_kernel, tiles_per_seq=seq // TM)
    return pl.pallas_call(
        kern,
        grid=(m // TM, n // SUB),
        in_specs=[
            pl.BlockSpec((TM, d), lambda i, j: (i, 0), pipeline_mode=pl.Buffered(1)),
            pl.BlockSpec((1, d), lambda i, j: (0, 0)),
            pl.BlockSpec((d, SUB), lambda i, j: (0, j)),
            pl.BlockSpec((d, SUB), lambda i, j: (0, j)),
            pl.BlockSpec((cw.shape[0], SUB), lambda i, j: (0, j)),
            pl.BlockSpec((SUB, d), lambda i, j: (j, 0)),
        ],
        out_specs=pl.BlockSpec((TM, d), lambda i, j: (i, 0)),
        out_shape=jax.ShapeDtypeStruct((m, d), F32),
        scratch_shapes=[
            pltpu.VMEM((TM, d), BF16),
            pltpu.VMEM((n // SUB, SUBLANES, SUB), F32),
            pltpu.VMEM((TM + SUBLANES, SUB), F32),
        ],
        compiler_params=_compiler_params(("arbitrary", "arbitrary"), 56 * 1024 * 1024),
        name="conv_ffn",
    )(x, gain, wu, wg, cw, wd)


def _gdn_proj_kernel(x_ref, g_ref, w_ref, cw_ref, o_ref, h_ref, tail_ref, ext_ref,
                     *, tiles_per_seq, n_head_steps, n_conv_steps, head_dim):
    i = pl.program_id(0)
    j = pl.program_id(1)

    @pl.when(j == 0)
    def _():
        h_ref[...] = _rms_rows(x_ref[...], g_ref[...]).astype(BF16)

    n_pieces = o_ref.shape[1] // SUB
    first = (i % tiles_per_seq) == 0

    def conv_piece(t):
        cols = slice(t * SUB, (t + 1) * SUB)
        y = _dot(h_ref[...], w_ref[:, cols])
        return _silu(_causal_conv_tile(y, cw_ref[:, cols], tail_ref, ext_ref.at[t],
                                       j * n_pieces + t, first))

    @pl.when(j < 2 * n_head_steps)
    def _():
        scale = jnp.where(j < n_head_steps, head_dim ** -0.5, 1.0)
        for t in range(n_pieces):
            _head_norm_store(conv_piece(t), o_ref, t * SUB, head_dim, scale, mean=False)

    @pl.when((j >= 2 * n_head_steps) & (j < n_conv_steps))
    def _():
        for t in range(n_pieces):
            o_ref[:, t * SUB:(t + 1) * SUB] = conv_piece(t).astype(o_ref.dtype)

    @pl.when(j >= n_conv_steps)
    def _():
        for t in range(n_pieces):
            cols = slice(t * SUB, (t + 1) * SUB)
            o_ref[:, cols] = _dot(h_ref[...], w_ref[:, cols]).astype(o_ref.dtype)


def _gdn_proj(x, gain, w, cw, seq, key_dim, head_dim):
    m, d = x.shape
    n = w.shape[1]
    n_conv_steps = cw.shape[1] // TN
    n_pieces = TN // SUB
    kern = functools.partial(
        _gdn_proj_kernel, tiles_per_seq=seq // TM, n_head_steps=key_dim // TN,
        n_conv_steps=n_conv_steps, head_dim=head_dim)
    return pl.pallas_call(
        kern,
        grid=(m // TM, n // TN),
        in_specs=[
            pl.BlockSpec((TM, d), lambda i, j: (i, 0)),
            pl.BlockSpec((1, d), lambda i, j: (0, 0)),
            pl.BlockSpec((d, TN), lambda i, j: (0, j)),
            pl.BlockSpec((cw.shape[0], TN), lambda i, j: (0, jnp.minimum(j, n_conv_steps - 1))),
        ],
        out_specs=pl.BlockSpec((TM, TN), lambda i, j: (i, j)),
        out_shape=jax.ShapeDtypeStruct((m, n), BF16),
        scratch_shapes=[
            pltpu.VMEM((TM, d), BF16),
            pltpu.VMEM((n_conv_steps * n_pieces, SUBLANES, SUB), F32),
            pltpu.VMEM((n_pieces, TM + SUBLANES, SUB), F32),
        ],
        compiler_params=_compiler_params(("arbitrary", "arbitrary"), 48 * 1024 * 1024),
        name="gdn_proj",
    )(x, gain, w, cw)


def _gdn_gate_kernel(x_ref, g_ref, w_ref, alog_ref, dtb_ref, o_ref, *, n_heads):
    h = _rms_rows(x_ref[...], g_ref[...]).astype(BF16)
    y = _dot(h, w_ref[...])
    is_decay = lax.broadcasted_iota(jnp.int32, (1, y.shape[1]), 1) < n_heads
    g = -jnp.exp(alog_ref[...]) * jax.nn.softplus(y + dtb_ref[...])
    vals = jnp.where(is_decay, g, jax.nn.sigmoid(y))
    ri = lax.broadcasted_iota(jnp.int32, (CHUNK, CHUNK), 0)
    ci = lax.broadcasted_iota(jnp.int32, (CHUNK, CHUNK), 1)
    tri = (ri >= ci).astype(F32)
    for c in range(y.shape[0] // CHUNK):
        rows = slice(c * CHUNK, (c + 1) * CHUNK)
        v = vals[rows, :]
        cum = jnp.dot(tri, v, precision=lax.Precision.HIGHEST, preferred_element_type=F32)
        o_ref[rows, :] = jnp.where(is_decay, cum, v)


def _gdn_gates(x, gain, w_ab, alog, dtb, n_heads):
    m, d = x.shape
    kern = functools.partial(_gdn_gate_kernel, n_heads=n_heads)
    return pl.pallas_call(
        kern,
        grid=(m // TM,),
        in_specs=[
            pl.BlockSpec((TM, d), lambda i: (i, 0)),
            pl.BlockSpec((1, d), lambda i: (0, 0)),
            pl.BlockSpec((d, LANES), lambda i: (0, 0)),
            pl.BlockSpec((1, LANES), lambda i: (0, 0)),
            pl.BlockSpec((1, LANES), lambda i: (0, 0)),
        ],
        out_specs=pl.BlockSpec((TM, LANES), lambda i: (i, 0)),
        out_shape=jax.ShapeDtypeStruct((m, LANES), F32),
        compiler_params=_compiler_params(("arbitrary",), 32 * 1024 * 1024),
        name="gdn_gates",
    )(x, gain, w_ab, alog, dtb)


def _mm_resid_kernel(a_ref, w_ref, r_ref, o_ref):
    o_ref[...] = r_ref[...] + _dot(a_ref[...], w_ref[...])


def _mm_resid(a, w, resid):
    m, k = a.shape
    n = w.shape[1]
    tn = TN if k * TN * 2 <= 4 * 1024 * 1024 else SUB
    return pl.pallas_call(
        _mm_resid_kernel,
        grid=(m // TM, n // tn),
        in_specs=[
            pl.BlockSpec((TM, k), lambda i, j: (i, 0)),
            pl.BlockSpec((k, tn), lambda i, j: (0, j)),
            pl.BlockSpec((TM, tn), lambda i, j: (i, j)),
        ],
        out_specs=pl.BlockSpec((TM, tn), lambda i, j: (i, j)),
        out_shape=jax.ShapeDtypeStruct((m, n), F32),
        compiler_params=_compiler_params(("arbitrary", "arbitrary"), 48 * 1024 * 1024),
        name="mm_resid",
    )(a, w, resid)


def _bias_table_kernel(rb_ref, idx_ref, add_ref, o_ref):
    nrel = rb_ref.shape[1]
    r = lax.broadcasted_iota(jnp.int32, (nrel, idx_ref.shape[1]), 0)
    onehot = (r == idx_ref[...]).astype(F32)
    o_ref[...] = jnp.dot(rb_ref[...], onehot, precision=lax.Precision.HIGHEST,
                         preferred_element_type=F32) + add_ref[...]


def _band_layout(rows, keys):
    qpos = np.arange(rows)[:, None]
    kpos = np.arange(keys)[None, :] - (keys - rows)
    qc = qpos // CHUNK
    kc = np.floor_divide(kpos, CHUNK)
    in_band = (kc <= qc) & (kc >= qc - LEFT_CHUNKS)
    idx = np.clip(qpos - kpos, -(CHUNK - 1), MAX_REL) + (CHUNK - 1)
    idx = np.where(in_band, idx, 0).astype(np.int32)
    add = np.where(in_band, 0.0, MASK_VALUE).astype(np.float32)
    return idx.reshape(1, -1), add.reshape(1, -1)


def _bias_table(rel_bias, rows, keys):
    heads, nrel = rel_bias.shape
    nrel_pad = -(-nrel // LANES) * LANES
    rb = jnp.pad(rel_bias.astype(F32), ((0, 0), (0, nrel_pad - nrel)))
    idx, add = _band_layout(rows, keys)
    n = rows * keys
    tb = 2048
    out = pl.pallas_call(
        _bias_table_kernel,
        grid=(n // tb,),
        in_specs=[
            pl.BlockSpec((heads, nrel_pad), lambda i: (0, 0)),
            pl.BlockSpec((1, tb), lambda i: (0, i)),
            pl.BlockSpec((1, tb), lambda i: (0, i)),
        ],
        out_specs=pl.BlockSpec((heads, tb), lambda i: (0, i)),
        out_shape=jax.ShapeDtypeStruct((heads, n), F32),
        compiler_params=_compiler_params(("arbitrary",), 32 * 1024 * 1024),
        name="att_bias_table",
    )(rb, jnp.asarray(idx), jnp.asarray(add))
    return out.reshape(heads, rows, keys)


def _attn_kernel(q_ref, k0_ref, k1_ref, k2_ref, v0_ref, v1_ref, v2_ref, bias_ref, o_ref,
                 *, head_dim):
    qb = pl.program_id(2)
    rows = q_ref.shape[0]
    k_refs = (k0_ref, k1_ref, k2_ref)
    v_refs = (v0_ref, v1_ref, v2_ref)
    nblk = len(k_refs)
    for hh in range(q_ref.shape[1] // head_dim):
        sl = slice(hh * head_dim, (hh + 1) * head_dim)
        q = q_ref[:, sl]
        scores = []
        for t in range(nblk):
            s = _dot_nt(q, k_refs[t][:, sl]) + bias_ref[hh, :, t * rows:(t + 1) * rows]
            if t < nblk - 1:
                s = jnp.where(qb >= nblk - 1 - t, s, MASK_VALUE)
            scores.append(s)
        mx = scores[0].max(axis=-1, keepdims=True)
        for s in scores[1:]:
            mx = jnp.maximum(mx, s.max(axis=-1, keepdims=True))
        den = jnp.zeros_like(mx)
        acc = jnp.zeros((rows, head_dim), F32)
        for t in range(nblk):
            p = jnp.exp(scores[t] - mx)
            den = den + p.sum(axis=-1, keepdims=True)
            acc = acc + _dot(p.astype(BF16), v_refs[t][:, sl])
        o_ref[:, sl] = (acc / den).astype(o_ref.dtype)


def _band_attention(qkv, bias, batch, seq, d, head_dim):
    m = qkv.shape[0]
    r = ATT_ROWS
    gw = ATT_HEAD_GROUP * head_dim
    n_groups = d // gw
    nqb = seq // r
    nblk = bias.shape[2] // r

    def q_map(g, b, qb):
        return (b * nqb + qb, g)

    def kv_map(off, col0):
        def f(g, b, qb):
            return (b * nqb + jnp.maximum(qb - off, 0), col0 + g)
        return f

    k_specs = [pl.BlockSpec((r, gw), kv_map(nblk - 1 - t, n_groups)) for t in range(nblk)]
    v_specs = [pl.BlockSpec((r, gw), kv_map(nblk - 1 - t, 2 * n_groups)) for t in range(nblk)]
    kern = functools.partial(_attn_kernel, head_dim=head_dim)
    return pl.pallas_call(
        kern,
        grid=(n_groups, batch, nqb),
        in_specs=[pl.BlockSpec((r, gw), q_map)] + k_specs + v_specs + [
            pl.BlockSpec((ATT_HEAD_GROUP, r, nblk * r), lambda g, b, qb: (g, 0, 0))],
        out_specs=pl.BlockSpec((r, gw), q_map),
        out_shape=jax.ShapeDtypeStruct((m, d), BF16),
        compiler_params=_compiler_params(("arbitrary", "arbitrary", "arbitrary"),
                                         32 * 1024 * 1024),
        name="band_attention",
    )(qkv, qkv, qkv, qkv, qkv, qkv, qkv, bias)


def _pool_kernel(x_ref, g_ref, w_ref, sc_ref, o_ref, ext_ref, *, tiles_per_seq, windows):
    i = pl.program_id(0)
    tm, d = x_ref.shape
    halo = max(windows)
    x = x_ref[...]
    h = _rms_rows(x, g_ref[...])
    first = (i % tiles_per_seq) == 0

    @pl.when(first)
    def _():
        ext_ref[0:halo, :] = jnp.zeros((halo, d), F32)

    @pl.when(jnp.logical_not(first))
    def _():
        ext_ref[0:halo, :] = ext_ref[tm:tm + halo, :]

    ext_ref[halo:halo + tm, :] = h
    pos = (i % tiles_per_seq) * tm + lax.broadcasted_iota(jnp.int32, (tm, 1), 0)
    dg = d // len(windows)
    for g, w in enumerate(windows):
        cs = slice(g * dg, (g + 1) * dg)
        hg = h[:, cs]
        acc = hg
        for s in range(1, w):
            acc = acc + ext_ref[pl.ds(halo - s, tm), cs]
        cnt = jnp.minimum(pos + 1, w).astype(F32)
        pooled = acc / cnt - hg
        y = _dot(pooled.astype(BF16), w_ref[g]) * sc_ref[:, cs]
        o_ref[:, cs] = x[:, cs] + y


def _pool_mixer(x, gain, pool_w, pool_scale, seq):
    m, d = x.shape
    g, dg, _ = pool_w.shape
    halo = max(POOL_WINDOWS)
    kern = functools.partial(_pool_kernel, tiles_per_seq=seq // POOL_TM, windows=POOL_WINDOWS)
    return pl.pallas_call(
        kern,
        grid=(m // POOL_TM,),
        in_specs=[
            pl.BlockSpec((POOL_TM, d), lambda i: (i, 0)),
            pl.BlockSpec((1, d), lambda i: (0, 0)),
            pl.BlockSpec((g, dg, dg), lambda i: (0, 0, 0)),
            pl.BlockSpec((1, d), lambda i: (0, 0)),
        ],
        out_specs=pl.BlockSpec((POOL_TM, d), lambda i: (i, 0)),
        out_shape=jax.ShapeDtypeStruct((m, d), F32),
        scratch_shapes=[pltpu.VMEM((POOL_TM + halo, d), F32)],
        compiler_params=_compiler_params(("arbitrary",), 40 * 1024 * 1024),
        name="pool_mixer",
    )(x, gain, pool_w, pool_scale)


def _mm_inv(p, q):
    return _dot(p.astype(BF16), q.astype(BF16))


def _unit_lower_inverse(mats, ri, ci, eye):
    base = SUBLANES
    bmask = (ri // base) == (ci // base)
    a0 = [jnp.where(bmask, a, 0.0) for a in mats]
    p = [_mm_inv(m, m) for m in a0]
    x = [eye - m for m in a0]
    x = [xi + _mm_inv(xi, pi) for xi, pi in zip(x, p)]
    p = [_mm_inv(pi, pi) for pi in p]
    x = [xi + _mm_inv(xi, pi) for xi, pi in zip(x, p)]
    s = base
    while s < CHUNK:
        nmask = ((ri // (2 * s)) == (ci // (2 * s))) & ((ri // s) != (ci // s))
        n = [jnp.where(nmask, a, 0.0) for a in mats]
        y = [_mm_inv(ni, xi) for ni, xi in zip(n, x)]
        x = [xi - _mm_inv(xi, yi) for xi, yi in zip(x, y)]
        s *= 2
    return x


def _gdn_kernel(q_ref, k_ref, v_ref, gate_ref, gb_ref, rows_ref, og_ref, o_ref,
                s_ref, gq_ref, ho_ref, *, n_chunks, n_heads, dv, unroll):
    hk = pl.program_id(1)

    @pl.when(pl.program_id(2) == 0)
    def _():
        s_ref[...] = jnp.zeros(s_ref.shape, F32)

    c2 = 2 * CHUNK
    ri = lax.broadcasted_iota(jnp.int32, (c2, c2), 0)
    ci = lax.broadcasted_iota(jnp.int32, (c2, c2), 1)
    same = (ri // CHUNK) == (ci // CHUNK)
    causal = same & (ri >= ci)
    strict = same & (ri > ci)
    eye = (ri == ci).astype(F32)
    head0_lanes = ci < CHUNK
    lane = lax.broadcasted_iota(jnp.int32, (CHUNK, LANES), 1)
    og = og_ref[...]

    def col(gb, idx):
        return jnp.sum(jnp.where(lane == idx, gb, 0.0), axis=1, keepdims=True)

    def prepare(cs):
        n = range(len(cs))
        r0 = [pl.multiple_of(c * CHUNK, CHUNK) for c in cs]
        q = [q_ref[pl.ds(r, CHUNK), :] for r in r0]
        k = [k_ref[pl.ds(r, CHUNK), :] for r in r0]
        v = [v_ref[pl.ds(r, CHUNK), :] for r in r0]
        gb = [gb_ref[pl.ds(r, CHUNK), :] for r in r0]
        rows = [rows_ref[0, c] for c in cs]
        grow = [x[0:1] for x in rows]
        brow = [x[1:2] for x in rows]
        glast = [x[2:3] for x in rows]
        gcol = [jnp.concatenate([col(g, 2 * hk), col(g, 2 * hk + 1)], axis=0) for g in gb]
        bcol = [jnp.concatenate([col(g, n_heads + 2 * hk), col(g, n_heads + 2 * hk + 1)], axis=0)
                for g in gb]

        k2 = [jnp.concatenate([x, x], axis=0) for x in k]
        qkk = [_dot_nt(jnp.concatenate([q[i], k[i]], axis=0), k2[i]) for i in n]
        qk = [jnp.concatenate([x[:CHUNK], x[:CHUNK]], axis=0) for x in qkk]
        kk = [jnp.concatenate([x[CHUNK:], x[CHUNK:]], axis=0) for x in qkk]
        decay = [jnp.where(causal, jnp.exp(jnp.where(causal, gcol[i] - grow[i], 0.0)), 0.0)
                 for i in n]
        a = [jnp.where(strict, bcol[i] * kk[i] * decay[i], 0.0) for i in n]
        attn = [qk[i] * decay[i] for i in n]
        t = _unit_lower_inverse(a, ri, ci, eye)

        v2 = [jnp.concatenate([x[:, :dv], x[:, dv:]], axis=0) for x in v]
        u = [_dot((t[i] * brow[i]).astype(BF16), v2[i]) for i in n]
        w = [_dot((t[i] * (brow[i] * jnp.exp(grow[i]))).astype(BF16), k2[i]) for i in n]
        wu = [jnp.concatenate([w[i], u[i]], axis=1).astype(BF16) for i in n]

        kte = [k2[i].astype(F32).T * jnp.exp(glast[i] - grow[i]) for i in n]
        lhs = [jnp.concatenate([
            jnp.where(head0_lanes, kte[i], 0.0),
            jnp.where(head0_lanes, 0.0, kte[i]),
            attn[i]], axis=0).astype(BF16) for i in n]
        r = [_dot(lhs[i], wu[i]) for i in n]
        for i in n:
            q2f = jnp.concatenate([q[i], q[i]], axis=0).astype(F32)
            qmat = q2f * jnp.exp(gcol[i]) - r[i][2 * c2:, :dv]
            omat = r[i][2 * c2:, dv:]
            gq_ref[cs[i]] = jnp.concatenate([
                -r[i][:c2, :dv], qmat[:CHUNK], -r[i][c2:2 * c2, :dv], qmat[CHUNK:]],
                axis=0).astype(BF16)
            ho_ref[cs[i]] = jnp.concatenate([
                r[i][:c2, dv:], omat[:CHUNK], r[i][c2:2 * c2, dv:], omat[CHUNK:]], axis=0)

    def prepare_group(cg, carry):
        prepare([cg * unroll + cc for cc in range(unroll)])
        return carry

    lax.fori_loop(0, n_chunks // unroll, prepare_group, 0)

    blk = c2 + CHUNK

    def scan(c, states):
        r0 = pl.multiple_of(c * CHUNK, CHUNK)
        rows = rows_ref[0, c]
        gate = gate_ref[pl.ds(r0, CHUNK), :]
        new_states = []
        outs = []
        for j in range(2):
            s = states[j]
            r = _dot(gq_ref[c, j * blk:(j + 1) * blk, :], s.astype(BF16))
            ho = ho_ref[c, j * blk:(j + 1) * blk, :]
            dec = jnp.exp(rows[3 + j:4 + j])
            new_states.append(s * dec + r[:c2] + ho[:c2])
            o = r[c2:] + ho[c2:]
            on = o * lax.rsqrt(jnp.mean(o * o, axis=-1, keepdims=True) + EPS) * og
            outs.append(on * _silu(gate[:, j * dv:(j + 1) * dv].astype(F32)))
        o_ref[pl.ds(r0, CHUNK), :] = jnp.concatenate(outs, axis=1).astype(o_ref.dtype)
        return tuple(new_states)

    s0, s1 = lax.fori_loop(0, n_chunks, scan, (s_ref[0], s_ref[1]))
    s_ref[0] = s0
    s_ref[1] = s1


def _gdn_rows(gb, n_heads, n_kheads):
    m = gb.shape[0]
    nc = m // CHUNK
    rep = n_heads // n_kheads
    assert rep * CHUNK == LANES

    def stack(x):
        x = x.reshape(nc, CHUNK, n_kheads, rep).transpose(2, 0, 3, 1)
        return x.reshape(n_kheads, nc, rep * CHUNK)

    gc = gb[:, :n_heads].reshape(nc, CHUNK, n_heads)
    beta = gb[:, n_heads:2 * n_heads].reshape(nc, CHUNK, n_heads)
    last = jnp.broadcast_to(gc[:, CHUNK - 1:, :], gc.shape)
    last_k = gc[:, CHUNK - 1, :].reshape(nc, n_kheads, rep).transpose(1, 0, 2)
    per_head = [jnp.broadcast_to(last_k[:, :, r:r + 1], (n_kheads, nc, LANES)) for r in range(rep)]
    rows = [stack(gc), stack(beta), stack(last)] + per_head
    rows += [jnp.zeros_like(rows[0])] * (SUBLANES - len(rows))
    return jnp.stack(rows, axis=2)


def _gated_delta(proj, gb, rows, o_gain, batch, seq, n_kheads, n_heads, dk, dv):
    m = proj.shape[0]
    cb = GDN_CHUNKS_PER_STEP
    rb = cb * CHUNK
    ncb = seq // rb
    key_dim = n_kheads * dk
    val_dim = n_heads * dv
    vw = 2 * dv
    assert n_heads == 2 * n_kheads and dk == LANES and dv == LANES

    def rmap(b, hk, c):
        return b * ncb + c

    kern = functools.partial(_gdn_kernel, n_chunks=cb, n_heads=n_heads, dv=dv,
                             unroll=GDN_PREP_UNROLL)
    return pl.pallas_call(
        kern,
        grid=(batch, n_kheads, ncb),
        in_specs=[
            pl.BlockSpec((rb, dk), lambda b, hk, c: (rmap(b, hk, c), hk)),
            pl.BlockSpec((rb, dk), lambda b, hk, c: (rmap(b, hk, c), key_dim // dk + hk)),
            pl.BlockSpec((rb, vw), lambda b, hk, c: (rmap(b, hk, c), 2 * key_dim // vw + hk)),
            pl.BlockSpec((rb, vw), lambda b, hk, c: (rmap(b, hk, c),
                                                     (2 * key_dim + val_dim) // vw + hk)),
            pl.BlockSpec((rb, LANES), lambda b, hk, c: (rmap(b, hk, c), 0)),
            pl.BlockSpec((1, cb, SUBLANES, LANES), lambda b, hk, c: (hk, rmap(b, hk, c), 0, 0)),
            pl.BlockSpec((1, dv), lambda b, hk, c: (0, 0)),
        ],
        out_specs=pl.BlockSpec((rb, vw), lambda b, hk, c: (rmap(b, hk, c), hk)),
        out_shape=jax.ShapeDtypeStruct((m, val_dim), BF16),
        scratch_shapes=[
            pltpu.VMEM((2, dk, dv), F32),
            pltpu.VMEM((cb, 2 * (dk + CHUNK), dv), BF16),
            pltpu.VMEM((cb, 2 * (dk + CHUNK), dv), F32),
        ],
        compiler_params=_compiler_params(("arbitrary", "arbitrary", "arbitrary"),
                                         32 * 1024 * 1024),
        name="gated_delta",
    )(proj, proj, proj, proj, gb, rows, o_gain)


def _pad_cols(w, n):
    return jnp.pad(w, ((0, 0), (0, n - w.shape[1])))


def _attention_layer(x, gain, w_qkv, q_gain, k_gain, rel_bias, w_o, batch, seq):
    d = x.shape[1]
    head_dim = d // ATT_HEADS
    qk_gain = jnp.zeros((SUBLANES, head_dim), F32)
    qk_gain = qk_gain.at[0].set(q_gain.astype(F32) * head_dim ** -0.5).at[1].set(k_gain.astype(F32))
    qkv = _qkv_proj(x, gain, w_qkv.astype(BF16), qk_gain, head_dim)
    keys = ATT_ROWS + LEFT_CHUNKS * CHUNK
    bias = _bias_table(rel_bias, ATT_ROWS, keys)
    o = _band_attention(qkv, bias, batch, seq, d, head_dim)
    return _mm_resid(o, w_o.astype(BF16), x)


def _gdn_layer(x, gain, w_in, conv_w, a_log, dt_bias, o_gain, w_o, batch, seq):
    d = x.shape[1]
    dk = d // GDN_K_HEADS
    dv = dk
    key_dim = GDN_K_HEADS * dk
    val_dim = GDN_V_HEADS * dv
    conv_ch = 2 * key_dim + val_dim
    n_main = conv_ch + val_dim
    proj = _gdn_proj(x, gain, w_in[:, :n_main].astype(BF16), conv_w.astype(F32), seq, key_dim, dk)
    w_ab = _pad_cols(w_in[:, n_main:], LANES).astype(BF16)
    pad = LANES - GDN_V_HEADS
    alog = jnp.pad(a_log.astype(F32), (0, pad)).reshape(1, LANES)
    dtb = jnp.pad(dt_bias.astype(F32), (0, pad)).reshape(1, LANES)
    gb = _gdn_gates(x, gain, w_ab, alog, dtb, GDN_V_HEADS)
    rows = _gdn_rows(gb, GDN_V_HEADS, GDN_K_HEADS)
    o = _gated_delta(proj, gb, rows, o_gain.astype(F32).reshape(1, dv), batch, seq,
                     GDN_K_HEADS, GDN_V_HEADS, dk, dv)
    return _mm_resid(o, w_o.astype(BF16), x)


def _conv_ffn(x, gain, w_up, conv_w, w_down, seq):
    d_ff = conv_w.shape[1]
    d_ff_pad = -(-d_ff // SUB) * SUB
    wu = _pad_cols(w_up[:, :d_ff], d_ff_pad).astype(BF16)
    wg = _pad_cols(w_up[:, d_ff:], d_ff_pad).astype(BF16)
    cw = _pad_cols(conv_w.astype(F32), d_ff_pad)
    wd = jnp.pad(w_down, ((0, d_ff_pad - d_ff), (0, 0))).astype(BF16)
    return _conv_ffn_call(x, gain, wu, wg, cw, wd, seq)


def kernel(x, mix_norm, ffn_norm, att_w_qkv, att_q_gain, att_k_gain, att_rel_bias, att_w_o,
           pool_w, pool_scale, gdn_w_in, gdn_conv, gdn_a_log, gdn_dt_bias, gdn_o_gain,
           gdn_w_o, ffn_w_up, ffn_conv, ffn_w_down):
    batch, seq, d = x.shape
    depth = mix_norm.shape[0]
    assert seq % TM == 0 and seq % POOL_TM == 0 and seq % ATT_ROWS == 0
    assert seq % (GDN_CHUNKS_PER_STEP * CHUNK) == 0
    xf = x.reshape(batch * seq, d).astype(F32)
    for i in range(depth):
        kind = i % N_MIXERS
        j = i // N_MIXERS
        gain = mix_norm[i].astype(F32).reshape(1, d)
        if kind == 0:
            xf = _attention_layer(xf, gain, att_w_qkv[j], att_q_gain[j], att_k_gain[j],
                                  att_rel_bias[j], att_w_o[j], batch, seq)
        elif kind == 1:
            xf = _pool_mixer(xf, gain, pool_w[j].astype(BF16),
                             pool_scale[j].astype(F32).reshape(1, d), seq)
        else:
            xf = _gdn_layer(xf, gain, gdn_w_in[j], gdn_conv[j], gdn_a_log[j], gdn_dt_bias[j],
                            gdn_o_gain[j], gdn_w_o[j], batch, seq)
        fgain = ffn_norm[i].astype(F32).reshape(1, d)
        xf = _conv_ffn(xf, fgain, ffn_w_up[i], ffn_conv[i], ffn_w_down[i], seq)
    return xf.reshape(batch, seq, d).astype(x.dtype)
```

```python
import functools

import numpy as np
import jax
import jax.numpy as jnp
from jax import lax
from jax.experimental import pallas as pl
from jax.experimental.pallas import tpu as pltpu

F32 = jnp.float32
BF16 = jnp.bfloat16

CHUNK = 64
EPS = 1e-6
MASK_VALUE = -1e30
N_MIXERS = 3

ATT_HEADS = 16
LEFT_CHUNKS = 8
MAX_REL = 256

POOL_WINDOWS = (2, 4, 8, 16)

GDN_K_HEADS = 16
GDN_V_HEADS = 32

LANES = 128
SUBLANES = 8
V7X_VMEM_BYTES = 64 * 1024 * 1024

TM = 1024
TN = 1024
SUB = 512
POOL_TM = 512
ATT_ROWS = 256
ATT_HEAD_GROUP = 4
GDN_CHUNKS_PER_STEP = 8


def _compiler_params(semantics, vmem_bytes):
    assert vmem_bytes < V7X_VMEM_BYTES
    return pltpu.CompilerParams(dimension_semantics=semantics, vmem_limit_bytes=vmem_bytes)


def _rms_rows(x, gain):
    ms = jnp.mean(x * x, axis=-1, keepdims=True)
    return x * lax.rsqrt(ms + EPS) * gain


def _silu(x):
    return x * jax.nn.sigmoid(x)


def _dot(a, b):
    return jnp.dot(a, b, preferred_element_type=F32)


def _dot_nt(a, b):
    return lax.dot_general(a, b, (((1,), (1,)), ((), ())), preferred_element_type=F32)


def _causal_conv_tile(u, cw, tail_ref, ext_ref, t, first):
    tm = u.shape[0]
    k = cw.shape[0]
    prev = jnp.where(first, 0.0, tail_ref[t])
    ext_ref[0:SUBLANES, :] = prev
    ext_ref[SUBLANES:SUBLANES + tm, :] = u
    tail_ref[t] = u[tm - SUBLANES:, :]
    acc = u * cw[k - 1:k, :]
    for s in range(1, k):
        acc = acc + ext_ref[pl.ds(SUBLANES - s, tm), :] * cw[k - 1 - s:k - s, :]
    return acc


def _head_norm_store(y, o_ref, col0, head_dim, gain, mean):
    for hh in range(y.shape[1] // head_dim):
        sl = slice(hh * head_dim, (hh + 1) * head_dim)
        yh = y[:, sl]
        ss = jnp.sum(yh * yh, axis=-1, keepdims=True)
        r = lax.rsqrt((ss / head_dim if mean else ss) + EPS)
        o_ref[:, col0 + hh * head_dim:col0 + (hh + 1) * head_dim] = (yh * r * gain).astype(o_ref.dtype)


def _qkv_kernel(x_ref, g_ref, w_ref, qkg_ref, o_ref, h_ref, *, steps_per_kind, head_dim):
    j = pl.program_id(1)

    @pl.when(j == 0)
    def _():
        h_ref[...] = _rms_rows(x_ref[...], g_ref[...]).astype(BF16)

    kind = j // steps_per_kind
    pieces = range(o_ref.shape[1] // SUB)

    @pl.when(kind < 2)
    def _():
        gain = qkg_ref[pl.ds(kind, 1), :]
        for t in pieces:
            y = _dot(h_ref[...], w_ref[:, t * SUB:(t + 1) * SUB])
            _head_norm_store(y, o_ref, t * SUB, head_dim, gain, mean=True)

    @pl.when(kind == 2)
    def _():
        for t in pieces:
            cols = slice(t * SUB, (t + 1) * SUB)
            o_ref[:, cols] = _dot(h_ref[...], w_ref[:, cols]).astype(o_ref.dtype)


def _qkv_proj(x, gain, w, qk_gain, head_dim):
    m, d = x.shape
    n = w.shape[1]
    kern = functools.partial(_qkv_kernel, steps_per_kind=d // TN, head_dim=head_dim)
    return pl.pallas_call(
        kern,
        grid=(m // TM, n // TN),
        in_specs=[
            pl.BlockSpec((TM, d), lambda i, j: (i, 0)),
            pl.BlockSpec((1, d), lambda i, j: (0, 0)),
            pl.BlockSpec((d, TN), lambda i, j: (0, j)),
            pl.BlockSpec((SUBLANES, head_dim), lambda i, j: (0, 0)),
        ],
        out_specs=pl.BlockSpec((TM, TN), lambda i, j: (i, j)),
        out_shape=jax.ShapeDtypeStruct((m, n), BF16),
        scratch_shapes=[pltpu.VMEM((TM, d), BF16)],
        compiler_params=_compiler_params(("arbitrary", "arbitrary"), 48 * 1024 * 1024),
        name="qkv_proj",
    )(x, gain, w, qk_gain)


def _ffn_kernel(x_ref, g_ref, wu_ref, wg_ref, cw_ref, wd_ref, o_ref, h_ref, tail_ref, ext_ref,
                *, tiles_per_seq):
    i = pl.program_id(0)
    j = pl.program_id(1)

    @pl.when(j == 0)
    def _():
        x = x_ref[...]
        h_ref[...] = _rms_rows(x, g_ref[...]).astype(BF16)
        o_ref[...] = x

    h = h_ref[...]
    u = _dot(h, wu_ref[...])
    gate = _dot(h, wg_ref[...])
    first = (i % tiles_per_seq) == 0
    c = _causal_conv_tile(u, cw_ref[...], tail_ref, ext_ref, j, first)
    act = (_silu(c) * gate).astype(BF16)
    o_ref[...] += _dot(act, wd_ref[...])


def _conv_ffn_call(x, gain, wu, wg, cw, wd, seq):
    m, d = x.shape
    n = wu.shape[1]
    kern = functools.partial(_ffn_kernel, tiles_per_seq=seq // TM)
    return pl.pallas_call(
        kern,
        grid=(m // TM, n // SUB),
        in_specs=[
            pl.BlockSpec((TM, d), lambda i, j: (i, 0), pipeline_mode=pl.Buffered(1)),
            pl.BlockSpec((1, d), lambda i, j: (0, 0)),
            pl.BlockSpec((d, SUB), lambda i, j: (0, j)),
            pl.BlockSpec((d, SUB), lambda i, j: (0, j)),
            pl.BlockSpec((cw.shape[0], SUB), lambda i, j: (0, j)),
            pl.BlockSpec((SUB, d), lambda i, j: (j, 0)),
        ],
        out_specs=pl.BlockSpec((TM, d), lambda i, j: (i, 0)),
        out_shape=jax.ShapeDtypeStruct((m, d), F32),
        scratch_shapes=[
            pltpu.VMEM((TM, d), BF16),
            pltpu.VMEM((n // SUB, SUBLANES, SUB), F32),
            pltpu.VMEM((TM + SUBLANES, SUB), F32),
        ],
        compiler_params=_compiler_params(("arbitrary", "arbitrary"), 56 * 1024 * 1024),
        name="conv_ffn",
    )(x, gain, wu, wg, cw, wd)


def _gdn_proj_kernel(x_ref, g_ref, w_ref, cw_ref, o_ref, h_ref, tail_ref, ext_ref,
                     *, tiles_per_seq, n_head_steps, n_conv_steps, head_dim):
    i = pl.program_id(0)
    j = pl.program_id(1)

    @pl.when(j == 0)
    def _():
        h_ref[...] = _rms_rows(x_ref[...], g_ref[...]).astype(BF16)

    n_pieces = o_ref.shape[1] // SUB
    first = (i % tiles_per_seq) == 0

    def conv_piece(t):
        cols = slice(t * SUB, (t + 1) * SUB)
        y = _dot(h_ref[...], w_ref[:, cols])
        return _silu(_causal_conv_tile(y, cw_ref[:, cols], tail_ref, ext_ref.at[t],
                                       j * n_pieces + t, first))

    @pl.when(j < 2 * n_head_steps)
    def _():
        scale = jnp.where(j < n_head_steps, head_dim ** -0.5, 1.0)
        for t in range(n_pieces):
            _head_norm_store(conv_piece(t), o_ref, t * SUB, head_dim, scale, mean=False)

    @pl.when((j >= 2 * n_head_steps) & (j < n_conv_steps))
    def _():
        for t in range(n_pieces):
            o_ref[:, t * SUB:(t + 1) * SUB] = conv_piece(t).astype(o_ref.dtype)

    @pl.when(j >= n_conv_steps)
    def _():
        for t in range(n_pieces):
            cols = slice(t * SUB, (t + 1) * SUB)
            o_ref[:, cols] = _dot(h_ref[...], w_ref[:, cols]).astype(o_ref.dtype)


def _gdn_proj(x, gain, w, cw, seq, key_dim, head_dim):
    m, d = x.shape
    n = w.shape[1]
    n_conv_steps = cw.shape[1] // TN
    n_pieces = TN // SUB
    kern = functools.partial(
        _gdn_proj_kernel, tiles_per_seq=seq // TM, n_head_steps=key_dim // TN,
        n_conv_steps=n_conv_steps, head_dim=head_dim)
    return pl.pallas_call(
        kern,
        grid=(m // TM, n // TN),
        in_specs=[
            pl.BlockSpec((TM, d), lambda i, j: (i, 0)),
            pl.BlockSpec((1, d), lambda i, j: (0, 0)),
            pl.BlockSpec((d, TN), lambda i, j: (0, j)),
            pl.BlockSpec((cw.shape[0], TN), lambda i, j: (0, jnp.minimum(j, n_conv_steps - 1))),
        ],
        out_specs=pl.BlockSpec((TM, TN), lambda i, j: (i, j)),
        out_shape=jax.ShapeDtypeStruct((m, n), BF16),
        scratch_shapes=[
            pltpu.VMEM((TM, d), BF16),
            pltpu.VMEM((n_conv_steps * n_pieces, SUBLANES, SUB), F32),
            pltpu.VMEM((n_pieces, TM + SUBLANES, SUB), F32),
        ],
        compiler_params=_compiler_params(("arbitrary", "arbitrary"), 48 * 1024 * 1024),
        name="gdn_proj",
    )(x, gain, w, cw)


def _gdn_gate_kernel(x_ref, g_ref, w_ref, alog_ref, dtb_ref, o_ref, *, n_heads):
    h = _rms_rows(x_ref[...], g_ref[...]).astype(BF16)
    y = _dot(h, w_ref[...])
    is_decay = lax.broadcasted_iota(jnp.int32, (1, y.shape[1]), 1) < n_heads
    g = -jnp.exp(alog_ref[...]) * jax.nn.softplus(y + dtb_ref[...])
    vals = jnp.where(is_decay, g, jax.nn.sigmoid(y))
    ri = lax.broadcasted_iota(jnp.int32, (CHUNK, CHUNK), 0)
    ci = lax.broadcasted_iota(jnp.int32, (CHUNK, CHUNK), 1)
    tri = (ri >= ci).astype(F32)
    for c in range(y.shape[0] // CHUNK):
        rows = slice(c * CHUNK, (c + 1) * CHUNK)
        v = vals[rows, :]
        cum = jnp.dot(tri, v, precision=lax.Precision.HIGHEST, preferred_element_type=F32)
        o_ref[rows, :] = jnp.where(is_decay, cum, v)


def _gdn_gates(x, gain, w_ab, alog, dtb, n_heads):
    m, d = x.shape
    kern = functools.partial(_gdn_gate_kernel, n_heads=n_heads)
    return pl.pallas_call(
        kern,
        grid=(m // TM,),
        in_specs=[
            pl.BlockSpec((TM, d), lambda i: (i, 0)),
            pl.BlockSpec((1, d), lambda i: (0, 0)),
            pl.BlockSpec((d, LANES), lambda i: (0, 0)),
            pl.BlockSpec((1, LANES), lambda i: (0, 0)),
            pl.BlockSpec((1, LANES), lambda i: (0, 0)),
        ],
        out_specs=pl.BlockSpec((TM, LANES), lambda i: (i, 0)),
        out_shape=jax.ShapeDtypeStruct((m, LANES), F32),
        compiler_params=_compiler_params(("arbitrary",), 32 * 1024 * 1024),
        name="gdn_gates",
    )(x, gain, w_ab, alog, dtb)


def _mm_resid_kernel(a_ref, w_ref, r_ref, o_ref):
    o_ref[...] = r_ref[...] + _dot(a_ref[...], w_ref[...])


def _mm_resid(a, w, resid):
    m, k = a.shape
    n = w.shape[1]
    tn = TN if k * TN * 2 <= 4 * 1024 * 1024 else SUB
    return pl.pallas_call(
        _mm_resid_kernel,
        grid=(m // TM, n // tn),
        in_specs=[
            pl.BlockSpec((TM, k), lambda i, j: (i, 0)),
            pl.BlockSpec((k, tn), lambda i, j: (0, j)),
            pl.BlockSpec((TM, tn), lambda i, j: (i, j)),
        ],
        out_specs=pl.BlockSpec((TM, tn), lambda i, j: (i, j)),
        out_shape=jax.ShapeDtypeStruct((m, n), F32),
        compiler_params=_compiler_params(("arbitrary", "arbitrary"), 48 * 1024 * 1024),
        name="mm_resid",
    )(a, w, resid)


def _bias_table_kernel(rb_ref, idx_ref, add_ref, o_ref):
    nrel = rb_ref.shape[1]
    r = lax.broadcasted_iota(jnp.int32, (nrel, idx_ref.shape[1]), 0)
    onehot = (r == idx_ref[...]).astype(F32)
    o_ref[...] = jnp.dot(rb_ref[...], onehot, precision=lax.Precision.HIGHEST,
                         preferred_element_type=F32) + add_ref[...]


def _band_layout(rows, keys):
    qpos = np.arange(rows)[:, None]
    kpos = np.arange(keys)[None, :] - (keys - rows)
    qc = qpos // CHUNK
    kc = np.floor_divide(kpos, CHUNK)
    in_band = (kc <= qc) & (kc >= qc - LEFT_CHUNKS)
    idx = np.clip(qpos - kpos, -(CHUNK - 1), MAX_REL) + (CHUNK - 1)
    idx = np.where(in_band, idx, 0).astype(np.int32)
    add = np.where(in_band, 0.0, MASK_VALUE).astype(np.float32)
    return idx.reshape(1, -1), add.reshape(1, -1)


def _bias_table(rel_bias, rows, keys):
    heads, nrel = rel_bias.shape
    nrel_pad = -(-nrel // LANES) * LANES
    rb = jnp.pad(rel_bias.astype(F32), ((0, 0), (0, nrel_pad - nrel)))
    idx, add = _band_layout(rows, keys)
    n = rows * keys
    tb = 2048
    out = pl.pallas_call(
        _bias_table_kernel,
        grid=(n // tb,),
        in_specs=[
            pl.BlockSpec((heads, nrel_pad), lambda i: (0, 0)),
            pl.BlockSpec((1, tb), lambda i: (0, i)),
            pl.BlockSpec((1, tb), lambda i: (0, i)),
        ],
        out_specs=pl.BlockSpec((heads, tb), lambda i: (0, i)),
        out_shape=jax.ShapeDtypeStruct((heads, n), F32),
        compiler_params=_compiler_params(("arbitrary",), 32 * 1024 * 1024),
        name="att_bias_table",
    )(rb, jnp.asarray(idx), jnp.asarray(add))
    return out.reshape(heads, rows, keys)


def _attn_kernel(q_ref, k0_ref, k1_ref, k2_ref, v0_ref, v1_ref, v2_ref, bias_ref, o_ref,
                 *, head_dim):
    qb = pl.program_id(2)
    rows = q_ref.shape[0]
    k_refs = (k0_ref, k1_ref, k2_ref)
    v_refs = (v0_ref, v1_ref, v2_ref)
    nblk = len(k_refs)
    heads = range(q_ref.shape[1] // head_dim)
    sl = [slice(hh * head_dim, (hh + 1) * head_dim) for hh in heads]
    scores = []
    for hh in heads:
        q = q_ref[:, sl[hh]]
        row = []
        for t in range(nblk):
            s = _dot_nt(q, k_refs[t][:, sl[hh]]) + bias_ref[hh, :, t * rows:(t + 1) * rows]
            if t < nblk - 1:
                s = jnp.where(qb >= nblk - 1 - t, s, MASK_VALUE)
            row.append(s)
        scores.append(row)

    def fold(parts, op):
        m = functools.reduce(op, parts)
        return functools.reduce(op, [m[:, i:i + LANES] for i in range(0, m.shape[1], LANES)])

    mx = [fold(scores[hh], jnp.maximum).max(axis=-1, keepdims=True) for hh in heads]
    probs = [[jnp.exp(s - mx[hh]) for s in scores[hh]] for hh in heads]
    den = [fold(probs[hh], jnp.add).sum(axis=-1, keepdims=True) for hh in heads]
    acc = [sum(_dot(probs[hh][t].astype(BF16), v_refs[t][:, sl[hh]]) for t in range(nblk))
           for hh in heads]
    for hh in heads:
        o_ref[:, sl[hh]] = (acc[hh] / den[hh]).astype(o_ref.dtype)


def _band_attention(qkv, bias, batch, seq, d, head_dim):
    m = qkv.shape[0]
    r = ATT_ROWS
    gw = ATT_HEAD_GROUP * head_dim
    n_groups = d // gw
    nqb = seq // r
    nblk = bias.shape[2] // r

    def q_map(g, b, qb):
        return (b * nqb + qb, g)

    def kv_map(off, col0):
        def f(g, b, qb):
            return (b * nqb + jnp.maximum(qb - off, 0), col0 + g)
        return f

    k_specs = [pl.BlockSpec((r, gw), kv_map(nblk - 1 - t, n_groups)) for t in range(nblk)]
    v_specs = [pl.BlockSpec((r, gw), kv_map(nblk - 1 - t, 2 * n_groups)) for t in range(nblk)]
    kern = functools.partial(_attn_kernel, head_dim=head_dim)
    return pl.pallas_call(
        kern,
        grid=(n_groups, batch, nqb),
        in_specs=[pl.BlockSpec((r, gw), q_map)] + k_specs + v_specs + [
            pl.BlockSpec((ATT_HEAD_GROUP, r, nblk * r), lambda g, b, qb: (g, 0, 0))],
        out_specs=pl.BlockSpec((r, gw), q_map),
        out_shape=jax.ShapeDtypeStruct((m, d), BF16),
        compiler_params=_compiler_params(("arbitrary", "arbitrary", "arbitrary"),
                                         32 * 1024 * 1024),
        name="band_attention",
    )(qkv, qkv, qkv, qkv, qkv, qkv, qkv, bias)


def _pool_kernel(x_ref, g_ref, w_ref, sc_ref, o_ref, ext_ref, *, tiles_per_seq, windows):
    i = pl.program_id(0)
    tm, d = x_ref.shape
    halo = max(windows)
    x = x_ref[...]
    h = _rms_rows(x, g_ref[...])
    first = (i % tiles_per_seq) == 0

    @pl.when(first)
    def _():
        ext_ref[0:halo, :] = jnp.zeros((halo, d), F32)

    @pl.when(jnp.logical_not(first))
    def _():
        ext_ref[0:halo, :] = ext_ref[tm:tm + halo, :]

    ext_ref[halo:halo + tm, :] = h
    pos = (i % tiles_per_seq) * tm + lax.broadcasted_iota(jnp.int32, (tm, 1), 0)
    dg = d // len(windows)
    for g, w in enumerate(windows):
        cs = slice(g * dg, (g + 1) * dg)
        hg = h[:, cs]
        acc = hg
        for s in range(1, w):
            acc = acc + ext_ref[pl.ds(halo - s, tm), cs]
        cnt = jnp.minimum(pos + 1, w).astype(F32)
        pooled = acc / cnt - hg
        y = _dot(pooled.astype(BF16), w_ref[g]) * sc_ref[:, cs]
        o_ref[:, cs] = x[:, cs] + y


def _pool_mixer(x, gain, pool_w, pool_scale, seq):
    m, d = x.shape
    g, dg, _ = pool_w.shape
    halo = max(POOL_WINDOWS)
    kern = functools.partial(_pool_kernel, tiles_per_seq=seq // POOL_TM, windows=POOL_WINDOWS)
    return pl.pallas_call(
        kern,
        grid=(m // POOL_TM,),
        in_specs=[
            pl.BlockSpec((POOL_TM, d), lambda i: (i, 0)),
            pl.BlockSpec((1, d), lambda i: (0, 0)),
            pl.BlockSpec((g, dg, dg), lambda i: (0, 0, 0)),
            pl.BlockSpec((1, d), lambda i: (0, 0)),
        ],
        out_specs=pl.BlockSpec((POOL_TM, d), lambda i: (i, 0)),
        out_shape=jax.ShapeDtypeStruct((m, d), F32),
        scratch_shapes=[pltpu.VMEM((POOL_TM + halo, d), F32)],
        compiler_params=_compiler_params(("arbitrary",), 40 * 1024 * 1024),
        name="pool_mixer",
    )(x, gain, pool_w, pool_scale)


def _mm_inv(p, q):
    return _dot(p.astype(BF16), q.astype(BF16))


def _gdn_kernel(q_ref, k_ref, v_ref, gb_ref, rows_ref, gate_ref, srows_ref, og_ref, o_ref,
                s_ref, gq_ref, ho_ref, *, n_steps, n_kheads, n_blocks, n_chunks, n_heads, dv):
    t = pl.program_id(0)
    slot_p = t % 2
    slot_s = 1 - slot_p
    hk = (jnp.minimum(t, n_steps - 1) // n_blocks) % n_kheads
    scan_block = jnp.maximum(t - 1, 0) % n_blocks

    c2 = 2 * CHUNK
    blk = c2 + CHUNK
    ri = lax.broadcasted_iota(jnp.int32, (c2, c2), 0)
    ci = lax.broadcasted_iota(jnp.int32, (c2, c2), 1)
    same = (ri // CHUNK) == (ci // CHUNK)
    causal = same & (ri >= ci)
    strict = same & (ri > ci)
    eye = (ri == ci).astype(F32)
    head0_lanes = ci < CHUNK
    lane = lax.broadcasted_iota(jnp.int32, (CHUNK, LANES), 1)
    og = og_ref[...]
    n = range(n_chunks)

    def col(gb, idx):
        return jnp.sum(jnp.where(lane == idx, gb, 0.0), axis=1, keepdims=True)

    def prepare():
        r0 = [c * CHUNK for c in n]
        q = [q_ref[r:r + CHUNK, :] for r in r0]
        k = [k_ref[r:r + CHUNK, :] for r in r0]
        gb = [gb_ref[r:r + CHUNK, :] for r in r0]
        rows = [rows_ref[0, c] for c in n]
        grow = [x[0:1] for x in rows]
        brow = [x[1:2] for x in rows]
        glast = [x[2:3] for x in rows]
        gcol = [jnp.concatenate([col(g, 2 * hk), col(g, 2 * hk + 1)], axis=0) for g in gb]
        bcol = [jnp.concatenate([col(g, n_heads + 2 * hk), col(g, n_heads + 2 * hk + 1)], axis=0)
                for g in gb]
        yield
        k2 = [jnp.concatenate([x, x], axis=0) for x in k]
        qkk = [_dot_nt(jnp.concatenate([q[i], k[i]], axis=0), k2[i]) for i in n]
        yield
        qk = [jnp.concatenate([x[:CHUNK], x[:CHUNK]], axis=0) for x in qkk]
        kk = [jnp.concatenate([x[CHUNK:], x[CHUNK:]], axis=0) for x in qkk]
        decay = [jnp.where(causal, jnp.exp(jnp.where(causal, gcol[i] - grow[i], 0.0)), 0.0)
                 for i in n]
        a = [jnp.where(strict, bcol[i] * kk[i] * decay[i], 0.0) for i in n]
        attn = [qk[i] * decay[i] for i in n]
        yield
        base = SUBLANES
        bmask = (ri // base) == (ci // base)
        a0 = [jnp.where(bmask, m, 0.0) for m in a]
        p = [_mm_inv(m, m) for m in a0]
        x = [eye - m for m in a0]
        yield
        x = [xi + _mm_inv(xi, pi) for xi, pi in zip(x, p)]
        yield
        p = [_mm_inv(pi, pi) for pi in p]
        yield
        x = [xi + _mm_inv(xi, pi) for xi, pi in zip(x, p)]
        yield
        s = base
        while s < CHUNK:
            nmask = ((ri // (2 * s)) == (ci // (2 * s))) & ((ri // s) != (ci // s))
            y = [_mm_inv(jnp.where(nmask, ai, 0.0), xi) for ai, xi in zip(a, x)]
            yield
            x = [xi - _mm_inv(xi, yi) for xi, yi in zip(x, y)]
            yield
            s *= 2
        v = [v_ref[r:r + CHUNK, :] for r in r0]
        v2 = [jnp.concatenate([m[:, :dv], m[:, dv:]], axis=0) for m in v]
        u = [_dot((x[i] * brow[i]).astype(BF16), v2[i]) for i in n]
        yield
        w = [_dot((x[i] * (brow[i] * jnp.exp(grow[i]))).astype(BF16), k2[i]) for i in n]
        yield
        wu = [jnp.concatenate([w[i], u[i]], axis=1).astype(BF16) for i in n]
        kte = [k2[i].astype(F32).T * jnp.exp(glast[i] - grow[i]) for i in n]
        lhs = [jnp.concatenate([
            jnp.where(head0_lanes, kte[i], 0.0),
            jnp.where(head0_lanes, 0.0, kte[i]),
            attn[i]], axis=0).astype(BF16) for i in n]
        yield
        r = [_dot(lhs[i], wu[i]) for i in n]
        yield
        for i in n:
            q2f = jnp.concatenate([q[i], q[i]], axis=0).astype(F32)
            qmat = q2f * jnp.exp(gcol[i]) - r[i][2 * c2:, :dv]
            omat = r[i][2 * c2:, dv:]
            gq_ref[slot_p, i] = jnp.concatenate([
                -r[i][:c2, :dv], qmat[:CHUNK], -r[i][c2:2 * c2, :dv], qmat[CHUNK:]],
                axis=0).astype(BF16)
            ho_ref[slot_p, i] = jnp.concatenate([
                r[i][:c2, dv:], omat[:CHUNK], r[i][c2:2 * c2, dv:], omat[CHUNK:]], axis=0)
        yield

    def scan():
        fresh = scan_block == 0
        states = [jnp.where(fresh, 0.0, s_ref[j]) for j in range(2)]
        for c in n:
            rows = srows_ref[0, c]
            gate = gate_ref[c * CHUNK:(c + 1) * CHUNK, :]
            outs = []
            for j in range(2):
                s = states[j]
                r = _dot(gq_ref[slot_s, c, j * blk:(j + 1) * blk, :], s.astype(BF16))
                ho = ho_ref[slot_s, c, j * blk:(j + 1) * blk, :]
                dec = jnp.exp(rows[3 + j:4 + j])
                states[j] = s * dec + r[:c2] + ho[:c2]
                o = r[c2:] + ho[c2:]
                on = o * lax.rsqrt(jnp.mean(o * o, axis=-1, keepdims=True) + EPS) * og
                outs.append(on * _silu(gate[:, j * dv:(j + 1) * dv].astype(F32)))
            o_ref[c * CHUNK:(c + 1) * CHUNK, :] = jnp.concatenate(outs, axis=1).astype(o_ref.dtype)
            yield
        s_ref[0] = states[0]
        s_ref[1] = states[1]
        yield

    def run(generators, weights):
        live = list(generators)
        while any(g is not None for g in live):
            for i, g in enumerate(live):
                for _ in range(weights[i]):
                    if live[i] is not None and next(live[i], "done") == "done":
                        live[i] = None

    @pl.when(t == 0)
    def _():
        s_ref[...] = jnp.zeros(s_ref.shape, F32)
        run([prepare()], [1])

    @pl.when((t > 0) & (t < n_steps))
    def _():
        run([prepare(), scan()], [2, 1])

    @pl.when(t == n_steps)
    def _():
        run([scan()], [1])


def _gdn_rows(gb, n_heads, n_kheads):
    m = gb.shape[0]
    nc = m // CHUNK
    rep = n_heads // n_kheads
    assert rep * CHUNK == LANES

    def stack(x):
        x = x.reshape(nc, CHUNK, n_kheads, rep).transpose(2, 0, 3, 1)
        return x.reshape(n_kheads, nc, rep * CHUNK)

    gc = gb[:, :n_heads].reshape(nc, CHUNK, n_heads)
    beta = gb[:, n_heads:2 * n_heads].reshape(nc, CHUNK, n_heads)
    last = jnp.broadcast_to(gc[:, CHUNK - 1:, :], gc.shape)
    last_k = gc[:, CHUNK - 1, :].reshape(nc, n_kheads, rep).transpose(1, 0, 2)
    per_head = [jnp.broadcast_to(last_k[:, :, r:r + 1], (n_kheads, nc, LANES)) for r in range(rep)]
    rows = [stack(gc), stack(beta), stack(last)] + per_head
    rows += [jnp.zeros_like(rows[0])] * (SUBLANES - len(rows))
    return jnp.stack(rows, axis=2)


def _gated_delta(proj, gb, rows, o_gain, batch, seq, n_kheads, n_heads, dk, dv):
    m = proj.shape[0]
    cb = GDN_CHUNKS_PER_STEP
    rb = cb * CHUNK
    ncb = seq // rb
    key_dim = n_kheads * dk
    val_dim = n_heads * dv
    vw = 2 * dv
    n_steps = batch * n_kheads * ncb
    assert n_heads == 2 * n_kheads and dk == LANES and dv == LANES

    def decode(idx):
        return (idx // (n_kheads * ncb)) * ncb + idx % ncb, (idx // ncb) % n_kheads

    def prep(col0):
        def f(t):
            row, hk = decode(jnp.minimum(t, n_steps - 1))
            return row, col0 + hk
        return f

    def lagged(col0):
        def f(t):
            row, hk = decode(jnp.maximum(t - 1, 0))
            return row, col0 + hk
        return f

    def rows_map(f):
        def g(t):
            row, hk = f(t)
            return hk, row, 0, 0
        return g

    kern = functools.partial(_gdn_kernel, n_steps=n_steps, n_kheads=n_kheads, n_blocks=ncb,
                             n_chunks=cb, n_heads=n_heads, dv=dv)
    return pl.pallas_call(
        kern,
        grid=(n_steps + 1,),
        in_specs=[
            pl.BlockSpec((rb, dk), prep(0)),
            pl.BlockSpec((rb, dk), prep(key_dim // dk)),
            pl.BlockSpec((rb, vw), prep(2 * key_dim // vw)),
            pl.BlockSpec((rb, LANES), lambda t: (prep(0)(t)[0], 0)),
            pl.BlockSpec((1, cb, SUBLANES, LANES), rows_map(prep(0))),
            pl.BlockSpec((rb, vw), lagged((2 * key_dim + val_dim) // vw)),
            pl.BlockSpec((1, cb, SUBLANES, LANES), rows_map(lagged(0))),
            pl.BlockSpec((1, dv), lambda t: (0, 0)),
        ],
        out_specs=pl.BlockSpec((rb, vw), lagged(0)),
        out_shape=jax.ShapeDtypeStruct((m, val_dim), BF16),
        scratch_shapes=[
            pltpu.VMEM((2, dk, dv), F32),
            pltpu.VMEM((2, cb, 2 * (dk + CHUNK), dv), BF16),
            pltpu.VMEM((2, cb, 2 * (dk + CHUNK), dv), F32),
        ],
        compiler_params=_compiler_params(("arbitrary",), 32 * 1024 * 1024),
        name="gated_delta",
    )(proj, proj, proj, gb, rows, proj, rows, o_gain)


def _pad_cols(w, n):
    return jnp.pad(w, ((0, 0), (0, n - w.shape[1])))


def _attention_layer(x, gain, w_qkv, q_gain, k_gain, rel_bias, w_o, batch, seq):
    d = x.shape[1]
    head_dim = d // ATT_HEADS
    qk_gain = jnp.zeros((SUBLANES, head_dim), F32)
    qk_gain = qk_gain.at[0].set(q_gain.astype(F32) * head_dim ** -0.5).at[1].set(k_gain.astype(F32))
    qkv = _qkv_proj(x, gain, w_qkv.astype(BF16), qk_gain, head_dim)
    keys = ATT_ROWS + LEFT_CHUNKS * CHUNK
    bias = _bias_table(rel_bias, ATT_ROWS, keys)
    o = _band_attention(qkv, bias, batch, seq, d, head_dim)
    return _mm_resid(o, w_o.astype(BF16), x)


def _gdn_layer(x, gain, w_in, conv_w, a_log, dt_bias, o_gain, w_o, batch, seq):
    d = x.shape[1]
    dk = d // GDN_K_HEADS
    dv = dk
    key_dim = GDN_K_HEADS * dk
    val_dim = GDN_V_HEADS * dv
    conv_ch = 2 * key_dim + val_dim
    n_main = conv_ch + val_dim
    proj = _gdn_proj(x, gain, w_in[:, :n_main].astype(BF16), conv_w.astype(F32), seq, key_dim, dk)
    w_ab = _pad_cols(w_in[:, n_main:], LANES).astype(BF16)
    pad = LANES - GDN_V_HEADS
    alog = jnp.pad(a_log.astype(F32), (0, pad)).reshape(1, LANES)
    dtb = jnp.pad(dt_bias.astype(F32), (0, pad)).reshape(1, LANES)
    gb = _gdn_gates(x, gain, w_ab, alog, dtb, GDN_V_HEADS)
    rows = _gdn_rows(gb, GDN_V_HEADS, GDN_K_HEADS)
    o = _gated_delta(proj, gb, rows, o_gain.astype(F32).reshape(1, dv), batch, seq,
                     GDN_K_HEADS, GDN_V_HEADS, dk, dv)
    return _mm_resid(o, w_o.astype(BF16), x)


def _conv_ffn(x, gain, w_up, conv_w, w_down, seq):
    d_ff = conv_w.shape[1]
    d_ff_pad = -(-d_ff // SUB) * SUB
    wu = _pad_cols(w_up[:, :d_ff], d_ff_pad).astype(BF16)
    wg = _pad_cols(w_up[:, d_ff:], d_ff_pad).astype(BF16)
    cw = _pad_cols(conv_w.astype(F32), d_ff_pad)
    wd = jnp.pad(w_down, ((0, d_ff_pad - d_ff), (0, 0))).astype(BF16)
    return _conv_ffn_call(x, gain, wu, wg, cw, wd, seq)


def kernel(x, mix_norm, ffn_norm, att_w_qkv, att_q_gain, att_k_gain, att_rel_bias, att_w_o,
           pool_w, pool_scale, gdn_w_in, gdn_conv, gdn_a_log, gdn_dt_bias, gdn_o_gain,
           gdn_w_o, ffn_w_up, ffn_conv, ffn_w_down):
    batch, seq, d = x.shape
    depth = mix_norm.shape[0]
    assert seq % TM == 0 and seq % POOL_TM == 0 and seq % ATT_ROWS == 0
    assert seq % (GDN_CHUNKS_PER_STEP * CHUNK) == 0
    xf = x.reshape(batch * seq, d).astype(F32)
    for i in range(depth):
        kind = i % N_MIXERS
        j = i // N_MIXERS
        gain = mix_norm[i].astype(F32).reshape(1, d)
        if kind == 0:
            xf = _attention_layer(xf, gain, att_w_qkv[j], att_q_gain[j], att_k_gain[j],
                                  att_rel_bias[j], att_w_o[j], batch, seq)
        elif kind == 1:
            xf = _pool_mixer(xf, gain, pool_w[j].astype(BF16),
                             pool_scale[j].astype(F32).reshape(1, d), seq)
        else:
            xf = _gdn_layer(xf, gain, gdn_w_in[j], gdn_conv[j], gdn_a_log[j], gdn_dt_bias[j],
                            gdn_o_gain[j], gdn_w_o[j], batch, seq)
        fgain = ffn_norm[i].astype(F32).reshape(1, d)
        xf = _conv_ffn(xf, fgain, ffn_w_up[i], ffn_conv[i], ffn_w_down[i], seq)
    return xf.reshape(batch, seq, d).astype(x.dtype)
```

```python
import functools

import numpy as np
import jax
import jax.numpy as jnp
from jax import lax
from jax.experimental import pallas as pl
from jax.experimental.pallas import tpu as pltpu

F32 = jnp.float32
BF16 = jnp.bfloat16

CHUNK = 64
EPS = 1e-6
MASK_VALUE = -1e30
LOG2E = 1.4426950408889634
N_MIXERS = 3

ATT_HEADS = 16
LEFT_CHUNKS = 8
MAX_REL = 256

POOL_WINDOWS = (2, 4, 8, 16)

GDN_K_HEADS = 16
GDN_V_HEADS = 32

LANES = 128
SUBLANES = 8
V7X_VMEM_BYTES = 64 * 1024 * 1024

TM = 1024
TN = 1024
SUB = 512
POOL_TM = 512
ATT_ROWS = 256
ATT_HEAD_GROUP = 4
GDN_CHUNKS_PER_STEP = 16


def _compiler_params(semantics, vmem_bytes):
    assert vmem_bytes < V7X_VMEM_BYTES
    return pltpu.CompilerParams(dimension_semantics=semantics, vmem_limit_bytes=vmem_bytes)


def _rms_rows(x, gain):
    ms = jnp.mean(x * x, axis=-1, keepdims=True)
    return x * lax.rsqrt(ms + EPS) * gain


def _silu(x):
    return x * jax.nn.sigmoid(x)


def _dot(a, b):
    return jnp.dot(a, b, preferred_element_type=F32)


def _dot_nt(a, b):
    return lax.dot_general(a, b, (((1,), (1,)), ((), ())), preferred_element_type=F32)


def _causal_conv_tile(u, cw, tail_ref, ext_ref, t, first):
    tm = u.shape[0]
    k = cw.shape[0]
    prev = jnp.where(first, 0.0, tail_ref[t])
    ext_ref[0:SUBLANES, :] = prev
    ext_ref[SUBLANES:SUBLANES + tm, :] = u
    tail_ref[t] = u[tm - SUBLANES:, :]
    acc = u * cw[k - 1:k, :]
    for s in range(1, k):
        acc = acc + ext_ref[pl.ds(SUBLANES - s, tm), :] * cw[k - 1 - s:k - s, :]
    return acc


def _head_norm_store(y, o_ref, col0, head_dim, gain, mean):
    for hh in range(y.shape[1] // head_dim):
        sl = slice(hh * head_dim, (hh + 1) * head_dim)
        yh = y[:, sl]
        ss = jnp.sum(yh * yh, axis=-1, keepdims=True)
        r = lax.rsqrt((ss / head_dim if mean else ss) + EPS)
        o_ref[:, col0 + hh * head_dim:col0 + (hh + 1) * head_dim] = (yh * r * gain).astype(o_ref.dtype)


def _qkv_kernel(x_ref, g_ref, w_ref, qkg_ref, o_ref, h_ref, *, steps_per_kind, head_dim):
    j = pl.program_id(1)

    @pl.when(j == 0)
    def _():
        h_ref[...] = _rms_rows(x_ref[...], g_ref[...]).astype(BF16)

    kind = j // steps_per_kind
    pieces = range(o_ref.shape[1] // SUB)

    @pl.when(kind < 2)
    def _():
        gain = qkg_ref[pl.ds(kind, 1), :]
        for t in pieces:
            y = _dot(h_ref[...], w_ref[:, t * SUB:(t + 1) * SUB])
            _head_norm_store(y, o_ref, t * SUB, head_dim, gain, mean=True)

    @pl.when(kind == 2)
    def _():
        for t in pieces:
            cols = slice(t * SUB, (t + 1) * SUB)
            o_ref[:, cols] = _dot(h_ref[...], w_ref[:, cols]).astype(o_ref.dtype)


def _qkv_proj(x, gain, w, qk_gain, head_dim):
    m, d = x.shape
    n = w.shape[1]
    kern = functools.partial(_qkv_kernel, steps_per_kind=d // TN, head_dim=head_dim)
    return pl.pallas_call(
        kern,
        grid=(m // TM, n // TN),
        in_specs=[
            pl.BlockSpec((TM, d), lambda i, j: (i, 0)),
            pl.BlockSpec((1, d), lambda i, j: (0, 0)),
            pl.BlockSpec((d, TN), lambda i, j: (0, j)),
            pl.BlockSpec((SUBLANES, head_dim), lambda i, j: (0, 0)),
        ],
        out_specs=pl.BlockSpec((TM, TN), lambda i, j: (i, j)),
        out_shape=jax.ShapeDtypeStruct((m, n), BF16),
        scratch_shapes=[pltpu.VMEM((TM, d), BF16)],
        compiler_params=_compiler_params(("arbitrary", "arbitrary"), 48 * 1024 * 1024),
        name="qkv_proj",
    )(x, gain, w, qk_gain)


def _ffn_kernel(x_ref, g_ref, wu_ref, wg_ref, cw_ref, wd_ref, o_ref, h_ref, tail_ref, ext_ref,
                *, tiles_per_seq):
    i = pl.program_id(0)
    j = pl.program_id(1)

    @pl.when(j == 0)
    def _():
        x = x_ref[...]
        h_ref[...] = _rms_rows(x, g_ref[...]).astype(BF16)
        o_ref[...] = x

    h = h_ref[...]
    u = _dot(h, wu_ref[...])
    gate = _dot(h, wg_ref[...])
    first = (i % tiles_per_seq) == 0
    c = _causal_conv_tile(u, cw_ref[...], tail_ref, ext_ref, j, first)
    act = (_silu(c) * gate).astype(BF16)
    o_ref[...] += _dot(act, wd_ref[...])


def _conv_ffn_call(x, gain, wu, wg, cw, wd, seq):
    m, d = x.shape
    n = wu.shape[1]
    kern = functools.partial(_ffn_kernel, tiles_per_seq=seq // TM)
    return pl.pallas_call(
        kern,
        grid=(m // TM, n // SUB),
        in_specs=[
            pl.BlockSpec((TM, d), lambda i, j: (i, 0)),
            pl.BlockSpec((1, d), lambda i, j: (0, 0)),
            pl.BlockSpec((d, SUB), lambda i, j: (0, j)),
            pl.BlockSpec((d, SUB), lambda i, j: (0, j)),
            pl.BlockSpec((cw.shape[0], SUB), lambda i, j: (0, j)),
            pl.BlockSpec((SUB, d), lambda i, j: (j, 0)),
        ],
        out_specs=pl.BlockSpec((TM, d), lambda i, j: (i, 0)),
        out_shape=jax.ShapeDtypeStruct((m, d), F32),
        scratch_shapes=[
            pltpu.VMEM((TM, d), BF16),
            pltpu.VMEM((n // SUB, SUBLANES, SUB), F32),
            pltpu.VMEM((TM + SUBLANES, SUB), F32),
        ],
        compiler_params=_compiler_params(("arbitrary", "arbitrary"), 56 * 1024 * 1024),
        name="conv_ffn",
    )(x, gain, wu, wg, cw, wd)


def _gdn_proj_kernel(x_ref, g_ref, w_ref, cw_ref, o_ref, h_ref, tail_ref, ext_ref,
                     *, tiles_per_seq, n_head_steps, n_conv_steps, head_dim):
    i = pl.program_id(0)
    j = pl.program_id(1)

    @pl.when(j == 0)
    def _():
        h_ref[...] = _rms_rows(x_ref[...], g_ref[...]).astype(BF16)

    n_pieces = o_ref.shape[1] // SUB
    first = (i % tiles_per_seq) == 0

    def conv_piece(t):
        cols = slice(t * SUB, (t + 1) * SUB)
        y = _dot(h_ref[...], w_ref[:, cols])
        return _silu(_causal_conv_tile(y, cw_ref[:, cols], tail_ref, ext_ref.at[t],
                                       j * n_pieces + t, first))

    @pl.when(j < 2 * n_head_steps)
    def _():
        scale = jnp.where(j < n_head_steps, head_dim ** -0.5, 1.0)
        for t in range(n_pieces):
            _head_norm_store(conv_piece(t), o_ref, t * SUB, head_dim, scale, mean=False)

    @pl.when((j >= 2 * n_head_steps) & (j < n_conv_steps))
    def _():
        for t in range(n_pieces):
            o_ref[:, t * SUB:(t + 1) * SUB] = conv_piece(t).astype(o_ref.dtype)

    @pl.when(j >= n_conv_steps)
    def _():
        for t in range(n_pieces):
            cols = slice(t * SUB, (t + 1) * SUB)
            o_ref[:, cols] = _dot(h_ref[...], w_ref[:, cols]).astype(o_ref.dtype)


def _gdn_proj(x, gain, w, cw, seq, key_dim, head_dim):
    m, d = x.shape
    n = w.shape[1]
    n_conv_steps = cw.shape[1] // TN
    n_pieces = TN // SUB
    kern = functools.partial(
        _gdn_proj_kernel, tiles_per_seq=seq // TM, n_head_steps=key_dim // TN,
        n_conv_steps=n_conv_steps, head_dim=head_dim)
    return pl.pallas_call(
        kern,
        grid=(m // TM, n // TN),
        in_specs=[
            pl.BlockSpec((TM, d), lambda i, j: (i, 0)),
            pl.BlockSpec((1, d), lambda i, j: (0, 0)),
            pl.BlockSpec((d, TN), lambda i, j: (0, j)),
            pl.BlockSpec((cw.shape[0], TN), lambda i, j: (0, jnp.minimum(j, n_conv_steps - 1))),
        ],
        out_specs=pl.BlockSpec((TM, TN), lambda i, j: (i, j)),
        out_shape=jax.ShapeDtypeStruct((m, n), BF16),
        scratch_shapes=[
            pltpu.VMEM((TM, d), BF16),
            pltpu.VMEM((n_conv_steps * n_pieces, SUBLANES, SUB), F32),
            pltpu.VMEM((n_pieces, TM + SUBLANES, SUB), F32),
        ],
        compiler_params=_compiler_params(("arbitrary", "arbitrary"), 48 * 1024 * 1024),
        name="gdn_proj",
    )(x, gain, w, cw)


def _gdn_gate_kernel(x_ref, g_ref, w_ref, alog_ref, dtb_ref, o_ref, *, n_heads):
    h = _rms_rows(x_ref[...], g_ref[...]).astype(BF16)
    y = _dot(h, w_ref[...])
    is_decay = lax.broadcasted_iota(jnp.int32, (1, y.shape[1]), 1) < n_heads
    g = -jnp.exp(alog_ref[...]) * jax.nn.softplus(y + dtb_ref[...])
    vals = jnp.where(is_decay, g, jax.nn.sigmoid(y))
    ri = lax.broadcasted_iota(jnp.int32, (CHUNK, CHUNK), 0)
    ci = lax.broadcasted_iota(jnp.int32, (CHUNK, CHUNK), 1)
    tri = (ri >= ci).astype(F32)
    for c in range(y.shape[0] // CHUNK):
        rows = slice(c * CHUNK, (c + 1) * CHUNK)
        v = vals[rows, :]
        cum = jnp.dot(tri, v, precision=lax.Precision.HIGHEST, preferred_element_type=F32)
        o_ref[rows, :] = jnp.where(is_decay, cum, v)


def _gdn_gates(x, gain, w_ab, alog, dtb, n_heads):
    m, d = x.shape
    kern = functools.partial(_gdn_gate_kernel, n_heads=n_heads)
    return pl.pallas_call(
        kern,
        grid=(m // TM,),
        in_specs=[
            pl.BlockSpec((TM, d), lambda i: (i, 0)),
            pl.BlockSpec((1, d), lambda i: (0, 0)),
            pl.BlockSpec((d, LANES), lambda i: (0, 0)),
            pl.BlockSpec((1, LANES), lambda i: (0, 0)),
            pl.BlockSpec((1, LANES), lambda i: (0, 0)),
        ],
        out_specs=pl.BlockSpec((TM, LANES), lambda i: (i, 0)),
        out_shape=jax.ShapeDtypeStruct((m, LANES), F32),
        compiler_params=_compiler_params(("arbitrary",), 32 * 1024 * 1024),
        name="gdn_gates",
    )(x, gain, w_ab, alog, dtb)


def _mm_resid_kernel(a_ref, w_ref, r_ref, o_ref):
    o_ref[...] = r_ref[...] + _dot(a_ref[...], w_ref[...])


def _mm_resid(a, w, resid):
    m, k = a.shape
    n = w.shape[1]
    tm = TM // 2
    return pl.pallas_call(
        _mm_resid_kernel,
        grid=(m // tm,),
        in_specs=[
            pl.BlockSpec((tm, k), lambda i: (i, 0)),
            pl.BlockSpec((k, n), lambda i: (0, 0), pipeline_mode=pl.Buffered(1)),
            pl.BlockSpec((tm, n), lambda i: (i, 0)),
        ],
        out_specs=pl.BlockSpec((tm, n), lambda i: (i, 0)),
        out_shape=jax.ShapeDtypeStruct((m, n), F32),
        compiler_params=_compiler_params(("arbitrary",), 48 * 1024 * 1024),
        name="mm_resid",
    )(a, w, resid)


def _bias_table_kernel(rb_ref, idx_ref, add_ref, o_ref):
    nrel = rb_ref.shape[1]
    r = lax.broadcasted_iota(jnp.int32, (nrel, idx_ref.shape[1]), 0)
    onehot = (r == idx_ref[...]).astype(F32)
    o_ref[...] = jnp.dot(rb_ref[...], onehot, precision=lax.Precision.HIGHEST,
                         preferred_element_type=F32) * LOG2E + add_ref[...]


def _band_layout(rows, keys):
    qpos = np.arange(rows)[:, None]
    kpos = np.arange(keys)[None, :] - (keys - rows)
    qc = qpos // CHUNK
    kc = np.floor_divide(kpos, CHUNK)
    in_band = (kc <= qc) & (kc >= qc - LEFT_CHUNKS)
    idx = np.clip(qpos - kpos, -(CHUNK - 1), MAX_REL) + (CHUNK - 1)
    idx = np.where(in_band, idx, 0).astype(np.int32)
    add = np.where(in_band, 0.0, MASK_VALUE).astype(np.float32)
    return idx.reshape(1, -1), add.reshape(1, -1)


def _bias_table(rel_bias, rows, keys):
    heads, nrel = rel_bias.shape
    nrel_pad = -(-nrel // LANES) * LANES
    rb = jnp.pad(rel_bias.astype(F32), ((0, 0), (0, nrel_pad - nrel)))
    idx, add = _band_layout(rows, keys)
    n = rows * keys
    tb = 4096
    out = pl.pallas_call(
        _bias_table_kernel,
        grid=(n // tb,),
        in_specs=[
            pl.BlockSpec((heads, nrel_pad), lambda i: (0, 0)),
            pl.BlockSpec((1, tb), lambda i: (0, i)),
            pl.BlockSpec((1, tb), lambda i: (0, i)),
        ],
        out_specs=pl.BlockSpec((heads, tb), lambda i: (0, i)),
        out_shape=jax.ShapeDtypeStruct((heads, n), F32),
        compiler_params=_compiler_params(("arbitrary",), 32 * 1024 * 1024),
        name="att_bias_table",
    )(rb, jnp.asarray(idx), jnp.asarray(add))
    return out.reshape(heads, rows, keys)


def _attn_kernel(q_ref, k0_ref, k1_ref, k2_ref, v0_ref, v1_ref, v2_ref, bias_ref, o_ref,
                 *, head_dim):
    qb = pl.program_id(2)
    rows = q_ref.shape[0]
    k_refs = (k0_ref, k1_ref, k2_ref)
    v_refs = (v0_ref, v1_ref, v2_ref)
    nblk = len(k_refs)
    heads = range(q_ref.shape[1] // head_dim)
    sl = [slice(hh * head_dim, (hh + 1) * head_dim) for hh in heads]
    def qk(hh):
        q = q_ref[:, sl[hh]]
        row = []
        for t in range(nblk):
            s = _dot_nt(q, k_refs[t][:, sl[hh]]) + bias_ref[hh, :, t * rows:(t + 1) * rows]
            if t < nblk - 1:
                s = jnp.where(qb >= nblk - 1 - t, s, MASK_VALUE)
            row.append(s)
        return row

    def fold(parts, op):
        m = functools.reduce(op, parts)
        return functools.reduce(op, [m[:, i:i + LANES] for i in range(0, m.shape[1], LANES)])

    def softmax(sc):
        mx = fold(sc, jnp.maximum).max(axis=-1, keepdims=True)
        p = [jnp.exp2(s - mx) for s in sc]
        return p, fold(p, jnp.add).sum(axis=-1, keepdims=True)

    def pv(hh, p, den):
        acc = sum(_dot(p[t].astype(BF16), v_refs[t][:, sl[hh]]) for t in range(nblk))
        o_ref[:, sl[hh]] = (acc / den).astype(o_ref.dtype)

    sc = qk(0)
    for hh in heads:
        nxt = qk(hh + 1) if hh + 1 < len(heads) else None
        p, den = softmax(sc)
        pv(hh, p, den)
        sc = nxt


def _band_attention(qkv, bias, layer, batch, seq, d, head_dim):
    m = qkv.shape[0]
    r = ATT_ROWS
    gw = ATT_HEAD_GROUP * head_dim
    n_groups = d // gw
    nqb = seq // r
    nblk = bias.shape[2] // r

    def q_map(g, b, qb):
        return (b * nqb + qb, g)

    def kv_map(off, col0):
        def f(g, b, qb):
            return (b * nqb + jnp.maximum(qb - off, 0), col0 + g)
        return f

    k_specs = [pl.BlockSpec((r, gw), kv_map(nblk - 1 - t, n_groups)) for t in range(nblk)]
    v_specs = [pl.BlockSpec((r, gw), kv_map(nblk - 1 - t, 2 * n_groups)) for t in range(nblk)]
    kern = functools.partial(_attn_kernel, head_dim=head_dim)
    return pl.pallas_call(
        kern,
        grid=(n_groups, batch, nqb),
        in_specs=[pl.BlockSpec((r, gw), q_map)] + k_specs + v_specs + [
            pl.BlockSpec((ATT_HEAD_GROUP, r, nblk * r),
                         lambda g, b, qb: (layer * n_groups + g, 0, 0))],
        out_specs=pl.BlockSpec((r, gw), q_map),
        out_shape=jax.ShapeDtypeStruct((m, d), BF16),
        compiler_params=_compiler_params(("arbitrary", "arbitrary", "arbitrary"),
                                         32 * 1024 * 1024),
        name="band_attention",
    )(qkv, qkv, qkv, qkv, qkv, qkv, qkv, bias)


def _pool_kernel(x_ref, g_ref, w_ref, sc_ref, o_ref, ext_ref, *, tiles_per_seq, windows):
    i = pl.program_id(0)
    tm, d = x_ref.shape
    halo = max(windows)
    x = x_ref[...]
    h = _rms_rows(x, g_ref[...])
    first = (i % tiles_per_seq) == 0

    @pl.when(first)
    def _():
        ext_ref[0:halo, :] = jnp.zeros((halo, d), F32)

    @pl.when(jnp.logical_not(first))
    def _():
        ext_ref[0:halo, :] = ext_ref[tm:tm + halo, :]

    ext_ref[halo:halo + tm, :] = h
    pos = (i % tiles_per_seq) * tm + lax.broadcasted_iota(jnp.int32, (tm, 1), 0)
    dg = d // len(windows)
    for g, w in enumerate(windows):
        cs = slice(g * dg, (g + 1) * dg)
        hg = h[:, cs]
        acc = hg
        for s in range(1, w):
            acc = acc + ext_ref[pl.ds(halo - s, tm), cs]
        cnt = jnp.minimum(pos + 1, w).astype(F32)
        pooled = acc / cnt - hg
        y = _dot(pooled.astype(BF16), w_ref[g]) * sc_ref[:, cs]
        o_ref[:, cs] = x[:, cs] + y


def _pool_mixer(x, gain, pool_w, pool_scale, seq):
    m, d = x.shape
    g, dg, _ = pool_w.shape
    halo = max(POOL_WINDOWS)
    kern = functools.partial(_pool_kernel, tiles_per_seq=seq // POOL_TM, windows=POOL_WINDOWS)
    return pl.pallas_call(
        kern,
        grid=(m // POOL_TM,),
        in_specs=[
            pl.BlockSpec((POOL_TM, d), lambda i: (i, 0)),
            pl.BlockSpec((1, d), lambda i: (0, 0)),
            pl.BlockSpec((g, dg, dg), lambda i: (0, 0, 0)),
            pl.BlockSpec((1, d), lambda i: (0, 0)),
        ],
        out_specs=pl.BlockSpec((POOL_TM, d), lambda i: (i, 0)),
        out_shape=jax.ShapeDtypeStruct((m, d), F32),
        scratch_shapes=[pltpu.VMEM((POOL_TM + halo, d), F32)],
        compiler_params=_compiler_params(("arbitrary",), 40 * 1024 * 1024),
        name="pool_mixer",
    )(x, gain, pool_w, pool_scale)


def _mm_inv(p, q):
    return _dot(p.astype(BF16), q.astype(BF16))


def _gdn_kernel(q_ref, k_ref, v_ref, gb_ref, rows_ref, gate_ref, srows_ref, og_ref, o_ref,
                s_ref, gq_ref, ho_ref, *, n_steps, n_kheads, n_blocks, n_chunks, n_heads, dv):
    t = pl.program_id(0)
    slot_p = t % 2
    slot_s = 1 - slot_p
    hk = (jnp.minimum(t, n_steps - 1) // n_blocks) % n_kheads
    scan_block = jnp.maximum(t - 1, 0) % n_blocks

    c2 = 2 * CHUNK
    blk = c2 + CHUNK
    ri = lax.broadcasted_iota(jnp.int32, (c2, c2), 0)
    ci = lax.broadcasted_iota(jnp.int32, (c2, c2), 1)
    same = (ri // CHUNK) == (ci // CHUNK)
    causal = same & (ri >= ci)
    strict = same & (ri > ci)
    eye = (ri == ci).astype(F32)
    head0_lanes = ci < CHUNK
    lane = lax.broadcasted_iota(jnp.int32, (CHUNK, LANES), 1)
    og = og_ref[...]
    n = range(n_chunks)

    def col(gb, idx):
        return jnp.sum(jnp.where(lane == idx, gb, 0.0), axis=1, keepdims=True)

    def prepare():
        r0 = [c * CHUNK for c in n]
        q = [q_ref[r:r + CHUNK, :] for r in r0]
        k = [k_ref[r:r + CHUNK, :] for r in r0]
        gb = [gb_ref[r:r + CHUNK, :] for r in r0]
        rows = [rows_ref[0, c] for c in n]
        grow = [x[0:1] for x in rows]
        brow = [x[1:2] for x in rows]
        glast = [x[2:3] for x in rows]
        gcol = [jnp.concatenate([col(g, 2 * hk), col(g, 2 * hk + 1)], axis=0) for g in gb]
        bcol = [jnp.concatenate([col(g, n_heads + 2 * hk), col(g, n_heads + 2 * hk + 1)], axis=0)
                for g in gb]
        yield
        k2 = [jnp.concatenate([x, x], axis=0) for x in k]
        qkk = [_dot_nt(jnp.concatenate([q[i], k[i]], axis=0), k2[i]) for i in n]
        yield
        qk = [jnp.concatenate([x[:CHUNK], x[:CHUNK]], axis=0) for x in qkk]
        kk = [jnp.concatenate([x[CHUNK:], x[CHUNK:]], axis=0) for x in qkk]
        decay = [jnp.where(causal, jnp.exp(jnp.where(causal, gcol[i] - grow[i], 0.0)), 0.0)
                 for i in n]
        a = [jnp.where(strict, bcol[i] * kk[i] * decay[i], 0.0) for i in n]
        attn = [qk[i] * decay[i] for i in n]
        yield
        base = SUBLANES
        bmask = (ri // base) == (ci // base)
        a0 = [jnp.where(bmask, m, 0.0) for m in a]
        p = [_mm_inv(m, m) for m in a0]
        x = [eye - m for m in a0]
        yield
        x = [xi + _mm_inv(xi, pi) for xi, pi in zip(x, p)]
        yield
        p = [_mm_inv(pi, pi) for pi in p]
        yield
        x = [xi + _mm_inv(xi, pi) for xi, pi in zip(x, p)]
        yield
        s = base
        while s < CHUNK:
            nmask = ((ri // (2 * s)) == (ci // (2 * s))) & ((ri // s) != (ci // s))
            y = [_mm_inv(jnp.where(nmask, ai, 0.0), xi) for ai, xi in zip(a, x)]
            yield
            x = [xi - _mm_inv(xi, yi) for xi, yi in zip(x, y)]
            yield
            s *= 2
        v = [v_ref[r:r + CHUNK, :] for r in r0]
        v2 = [jnp.concatenate([m[:, :dv], m[:, dv:]], axis=0) for m in v]
        u = [_dot((x[i] * brow[i]).astype(BF16), v2[i]) for i in n]
        yield
        w = [_dot((x[i] * (brow[i] * jnp.exp(grow[i]))).astype(BF16), k2[i]) for i in n]
        yield
        wu = [jnp.concatenate([w[i], u[i]], axis=1).astype(BF16) for i in n]
        kte = [k2[i].astype(F32).T * jnp.exp(glast[i] - grow[i]) for i in n]
        lhs = [jnp.concatenate([
            jnp.where(head0_lanes, kte[i], 0.0),
            jnp.where(head0_lanes, 0.0, kte[i]),
            attn[i]], axis=0).astype(BF16) for i in n]
        yield
        r = [_dot(lhs[i], wu[i]) for i in n]
        yield
        for i in n:
            q2f = jnp.concatenate([q[i], q[i]], axis=0).astype(F32)
            qmat = q2f * jnp.exp(gcol[i]) - r[i][2 * c2:, :dv]
            omat = r[i][2 * c2:, dv:]
            gq_ref[slot_p, i] = jnp.concatenate([
                -r[i][:c2, :dv], qmat[:CHUNK], -r[i][c2:2 * c2, :dv], qmat[CHUNK:]],
                axis=0).astype(BF16)
            ho_ref[slot_p, i] = jnp.concatenate([
                r[i][:c2, dv:], omat[:CHUNK], r[i][c2:2 * c2, dv:], omat[CHUNK:]], axis=0)
        yield

    def scan():
        fresh = scan_block == 0
        states = [jnp.where(fresh, 0.0, s_ref[j]) for j in range(2)]
        for c in n:
            rows = srows_ref[0, c]
            gate = gate_ref[c * CHUNK:(c + 1) * CHUNK, :]
            outs = []
            for j in range(2):
                s = states[j]
                r = _dot(gq_ref[slot_s, c, j * blk:(j + 1) * blk, :], s.astype(BF16))
                ho = ho_ref[slot_s, c, j * blk:(j + 1) * blk, :]
                dec = jnp.exp(rows[3 + j:4 + j])
                states[j] = s * dec + r[:c2] + ho[:c2]
                o = r[c2:] + ho[c2:]
                on = o * lax.rsqrt(jnp.mean(o * o, axis=-1, keepdims=True) + EPS) * og
                outs.append(on * _silu(gate[:, j * dv:(j + 1) * dv].astype(F32)))
            o_ref[c * CHUNK:(c + 1) * CHUNK, :] = jnp.concatenate(outs, axis=1).astype(o_ref.dtype)
            yield
        s_ref[0] = states[0]
        s_ref[1] = states[1]
        yield

    def run(generators, weights):
        live = list(generators)
        while any(g is not None for g in live):
            for i, g in enumerate(live):
                for _ in range(weights[i]):
                    if live[i] is not None and next(live[i], "done") == "done":
                        live[i] = None

    @pl.when(t == 0)
    def _():
        s_ref[...] = jnp.zeros(s_ref.shape, F32)
        run([prepare()], [1])

    @pl.when((t > 0) & (t < n_steps))
    def _():
        run([prepare(), scan()], [2, 1])

    @pl.when(t == n_steps)
    def _():
        run([scan()], [1])


def _gdn_rows(gb, n_heads, n_kheads):
    m = gb.shape[0]
    nc = m // CHUNK
    rep = n_heads // n_kheads
    assert rep * CHUNK == LANES

    def stack(x):
        x = x.reshape(nc, CHUNK, n_kheads, rep).transpose(2, 0, 3, 1)
        return x.reshape(n_kheads, nc, rep * CHUNK)

    gc = gb[:, :n_heads].reshape(nc, CHUNK, n_heads)
    beta = gb[:, n_heads:2 * n_heads].reshape(nc, CHUNK, n_heads)
    last = jnp.broadcast_to(gc[:, CHUNK - 1:, :], gc.shape)
    last_k = gc[:, CHUNK - 1, :].reshape(nc, n_kheads, rep).transpose(1, 0, 2)
    per_head = [jnp.broadcast_to(last_k[:, :, r:r + 1], (n_kheads, nc, LANES)) for r in range(rep)]
    rows = [stack(gc), stack(beta), stack(last)] + per_head
    rows += [jnp.zeros_like(rows[0])] * (SUBLANES - len(rows))
    return jnp.stack(rows, axis=2)


def _gated_delta(proj, gb, rows, o_gain, batch, seq, n_kheads, n_heads, dk, dv):
    m = proj.shape[0]
    cb = GDN_CHUNKS_PER_STEP
    rb = cb * CHUNK
    ncb = seq // rb
    key_dim = n_kheads * dk
    val_dim = n_heads * dv
    vw = 2 * dv
    n_steps = batch * n_kheads * ncb
    assert n_heads == 2 * n_kheads and dk == LANES and dv == LANES

    def decode(idx):
        return (idx // (n_kheads * ncb)) * ncb + idx % ncb, (idx // ncb) % n_kheads

    def prep(col0):
        def f(t):
            row, hk = decode(jnp.minimum(t, n_steps - 1))
            return row, col0 + hk
        return f

    def lagged(col0):
        def f(t):
            row, hk = decode(jnp.maximum(t - 1, 0))
            return row, col0 + hk
        return f

    def rows_map(f):
        def g(t):
            row, hk = f(t)
            return hk, row, 0, 0
        return g

    kern = functools.partial(_gdn_kernel, n_steps=n_steps, n_kheads=n_kheads, n_blocks=ncb,
                             n_chunks=cb, n_heads=n_heads, dv=dv)
    return pl.pallas_call(
        kern,
        grid=(n_steps + 1,),
        in_specs=[
            pl.BlockSpec((rb, dk), prep(0)),
            pl.BlockSpec((rb, dk), prep(key_dim // dk)),
            pl.BlockSpec((rb, vw), prep(2 * key_dim // vw)),
            pl.BlockSpec((rb, LANES), lambda t: (prep(0)(t)[0], 0)),
            pl.BlockSpec((1, cb, SUBLANES, LANES), rows_map(prep(0))),
            pl.BlockSpec((rb, vw), lagged((2 * key_dim + val_dim) // vw)),
            pl.BlockSpec((1, cb, SUBLANES, LANES), rows_map(lagged(0))),
            pl.BlockSpec((1, dv), lambda t: (0, 0)),
        ],
        out_specs=pl.BlockSpec((rb, vw), lagged(0)),
        out_shape=jax.ShapeDtypeStruct((m, val_dim), BF16),
        scratch_shapes=[
            pltpu.VMEM((2, dk, dv), F32),
            pltpu.VMEM((2, cb, 2 * (dk + CHUNK), dv), BF16),
            pltpu.VMEM((2, cb, 2 * (dk + CHUNK), dv), F32),
        ],
        compiler_params=_compiler_params(("arbitrary",), 32 * 1024 * 1024),
        name="gated_delta",
    )(proj, proj, proj, gb, rows, proj, rows, o_gain)


def _pad_cols(w, n):
    return jnp.pad(w, ((0, 0), (0, n - w.shape[1])))


def _attention_layer(x, gain, w_qkv, q_gain, k_gain, bias, layer, w_o, batch, seq):
    d = x.shape[1]
    head_dim = d // ATT_HEADS
    qk_gain = jnp.zeros((SUBLANES, head_dim), F32)
    qk_gain = qk_gain.at[0].set(q_gain.astype(F32) * (head_dim ** -0.5 * LOG2E)).at[1].set(k_gain.astype(F32))
    qkv = _qkv_proj(x, gain, w_qkv.astype(BF16), qk_gain, head_dim)
    o = _band_attention(qkv, bias, layer, batch, seq, d, head_dim)
    return _mm_resid(o, w_o.astype(BF16), x)


def _gdn_layer(x, gain, w_in, conv_w, a_log, dt_bias, o_gain, w_o, batch, seq):
    d = x.shape[1]
    dk = d // GDN_K_HEADS
    dv = dk
    key_dim = GDN_K_HEADS * dk
    val_dim = GDN_V_HEADS * dv
    conv_ch = 2 * key_dim + val_dim
    n_main = conv_ch + val_dim
    proj = _gdn_proj(x, gain, w_in[:, :n_main].astype(BF16), conv_w.astype(F32), seq, key_dim, dk)
    w_ab = _pad_cols(w_in[:, n_main:], LANES).astype(BF16)
    pad = LANES - GDN_V_HEADS
    alog = jnp.pad(a_log.astype(F32), (0, pad)).reshape(1, LANES)
    dtb = jnp.pad(dt_bias.astype(F32), (0, pad)).reshape(1, LANES)
    gb = _gdn_gates(x, gain, w_ab, alog, dtb, GDN_V_HEADS)
    rows = _gdn_rows(gb, GDN_V_HEADS, GDN_K_HEADS)
    o = _gated_delta(proj, gb, rows, o_gain.astype(F32).reshape(1, dv), batch, seq,
                     GDN_K_HEADS, GDN_V_HEADS, dk, dv)
    return _mm_resid(o, w_o.astype(BF16), x)


def _conv_ffn(x, gain, w_up, conv_w, w_down, seq):
    d_ff = conv_w.shape[1]
    d_ff_pad = -(-d_ff // SUB) * SUB
    wu = _pad_cols(w_up[:, :d_ff], d_ff_pad).astype(BF16)
    wg = _pad_cols(w_up[:, d_ff:], d_ff_pad).astype(BF16)
    cw = _pad_cols(conv_w.astype(F32), d_ff_pad)
    wd = jnp.pad(w_down, ((0, d_ff_pad - d_ff), (0, 0))).astype(BF16)
    return _conv_ffn_call(x, gain, wu, wg, cw, wd, seq)


def kernel(x, mix_norm, ffn_norm, att_w_qkv, att_q_gain, att_k_gain, att_rel_bias, att_w_o,
           pool_w, pool_scale, gdn_w_in, gdn_conv, gdn_a_log, gdn_dt_bias, gdn_o_gain,
           gdn_w_o, ffn_w_up, ffn_conv, ffn_w_down):
    batch, seq, d = x.shape
    depth = mix_norm.shape[0]
    assert seq % TM == 0 and seq % POOL_TM == 0 and seq % ATT_ROWS == 0
    assert seq % (GDN_CHUNKS_PER_STEP * CHUNK) == 0
    xf = x.reshape(batch * seq, d).astype(F32)
    att_bias = _bias_table(att_rel_bias.reshape(-1, att_rel_bias.shape[-1]), ATT_ROWS,
                           ATT_ROWS + LEFT_CHUNKS * CHUNK)
    for i in range(depth):
        kind = i % N_MIXERS
        j = i // N_MIXERS
        gain = mix_norm[i].astype(F32).reshape(1, d)
        if kind == 0:
            xf = _attention_layer(xf, gain, att_w_qkv[j], att_q_gain[j], att_k_gain[j],
                                  att_bias, j, att_w_o[j], batch, seq)
        elif kind == 1:
            xf = _pool_mixer(xf, gain, pool_w[j].astype(BF16),
                             pool_scale[j].astype(F32).reshape(1, d), seq)
        else:
            xf = _gdn_layer(xf, gain, gdn_w_in[j], gdn_conv[j], gdn_a_log[j], gdn_dt_bias[j],
                            gdn_o_gain[j], gdn_w_o[j], batch, seq)
        fgain = ffn_norm[i].astype(F32).reshape(1, d)
        xf = _conv_ffn(xf, fgain, ffn_w_up[i], ffn_conv[i], ffn_w_down[i], seq)
    return xf.reshape(batch, seq, d).astype(x.dtype)
```

```python
import functools

import numpy as np
import jax
import jax.numpy as jnp
from jax import lax
from jax.experimental import pallas as pl
from jax.experimental.pallas import tpu as pltpu

F32 = jnp.float32
BF16 = jnp.bfloat16

CHUNK = 64
EPS = 1e-6
MASK_VALUE = -1e30
LOG2E = 1.4426950408889634
N_MIXERS = 3

ATT_HEADS = 16
LEFT_CHUNKS = 8
MAX_REL = 256

POOL_WINDOWS = (2, 4, 8, 16)

GDN_K_HEADS = 16
GDN_V_HEADS = 32

LANES = 128
SUBLANES = 8
V7X_VMEM_BYTES = 64 * 1024 * 1024

TM = 1024
TN = 2048
SUB = 512
POOL_TM = 512
ATT_ROWS = 256
ATT_HEAD_GROUP = 16
GDN_CHUNKS_PER_STEP = 16


def _compiler_params(semantics, vmem_bytes):
    assert vmem_bytes < V7X_VMEM_BYTES
    return pltpu.CompilerParams(dimension_semantics=semantics, vmem_limit_bytes=vmem_bytes)


def _rms_rows(x, gain):
    ms = jnp.mean(x * x, axis=-1, keepdims=True)
    return x * lax.rsqrt(ms + EPS) * gain


def _silu(x):
    return x * jax.nn.sigmoid(x)


def _dot(a, b):
    return jnp.dot(a, b, preferred_element_type=F32)


def _dot_nt(a, b):
    return lax.dot_general(a, b, (((1,), (1,)), ((), ())), preferred_element_type=F32)


def _causal_conv_tile(u, cw, tail_ref, ext_ref, t, first):
    tm = u.shape[0]
    k = cw.shape[0]
    prev = jnp.where(first, 0.0, tail_ref[t])
    ext_ref[0:SUBLANES, :] = prev
    ext_ref[SUBLANES:SUBLANES + tm, :] = u
    tail_ref[t] = u[tm - SUBLANES:, :]
    acc = u * cw[k - 1:k, :]
    for s in range(1, k):
        acc = acc + ext_ref[pl.ds(SUBLANES - s, tm), :] * cw[k - 1 - s:k - s, :]
    return acc


def _head_norm_store(y, o_ref, col0, head_dim, gain, mean):
    for hh in range(y.shape[1] // head_dim):
        sl = slice(hh * head_dim, (hh + 1) * head_dim)
        yh = y[:, sl]
        ss = jnp.sum(yh * yh, axis=-1, keepdims=True)
        r = lax.rsqrt((ss / head_dim if mean else ss) + EPS)
        o_ref[:, col0 + hh * head_dim:col0 + (hh + 1) * head_dim] = (yh * r * gain).astype(o_ref.dtype)


def _qkv_kernel(x_ref, g_ref, w_ref, qkg_ref, o_ref, h_ref, *, steps_per_kind, head_dim):
    j = pl.program_id(1)

    @pl.when(j == 0)
    def _():
        h_ref[...] = _rms_rows(x_ref[...], g_ref[...]).astype(BF16)

    kind = j // steps_per_kind
    pieces = range(o_ref.shape[1] // SUB)

    @pl.when(kind < 2)
    def _():
        gain = qkg_ref[pl.ds(kind, 1), :]
        for t in pieces:
            y = _dot(h_ref[...], w_ref[:, t * SUB:(t + 1) * SUB])
            _head_norm_store(y, o_ref, t * SUB, head_dim, gain, mean=True)

    @pl.when(kind == 2)
    def _():
        for t in pieces:
            cols = slice(t * SUB, (t + 1) * SUB)
            o_ref[:, cols] = _dot(h_ref[...], w_ref[:, cols]).astype(o_ref.dtype)


def _qkv_proj(x, gain, w, qk_gain, head_dim):
    m, d = x.shape
    n = w.shape[1]
    kern = functools.partial(_qkv_kernel, steps_per_kind=d // TN, head_dim=head_dim)
    return pl.pallas_call(
        kern,
        grid=(m // TM, n // TN),
        in_specs=[
            pl.BlockSpec((TM, d), lambda i, j: (i, 0)),
            pl.BlockSpec((1, d), lambda i, j: (0, 0)),
            pl.BlockSpec((d, TN), lambda i, j: (0, j)),
            pl.BlockSpec((SUBLANES, head_dim), lambda i, j: (0, 0)),
        ],
        out_specs=pl.BlockSpec((TM, TN), lambda i, j: (i, j)),
        out_shape=jax.ShapeDtypeStruct((m, n), BF16),
        scratch_shapes=[pltpu.VMEM((TM, d), BF16)],
        compiler_params=_compiler_params(("arbitrary", "arbitrary"), 56 * 1024 * 1024),
        name="qkv_proj",
    )(x, gain, w, qk_gain)


def _ffn_kernel(x_ref, g_ref, wu_ref, wg_ref, cw_ref, wd_ref, o_ref, h_ref, tail_ref, ext_ref,
                *, tiles_per_seq):
    i = pl.program_id(0)
    j = pl.program_id(1)

    @pl.when(j == 0)
    def _():
        x = x_ref[...]
        h_ref[...] = _rms_rows(x, g_ref[...]).astype(BF16)
        o_ref[...] = x

    h = h_ref[...]
    u = _dot(h, wu_ref[...])
    gate = _dot(h, wg_ref[...])
    first = (i % tiles_per_seq) == 0
    c = _causal_conv_tile(u, cw_ref[...], tail_ref, ext_ref, j, first)
    act = (_silu(c) * gate).astype(BF16)
    o_ref[...] += _dot(act, wd_ref[...])


def _conv_ffn_call(x, gain, wu, wg, cw, wd, seq):
    m, d = x.shape
    n = wu.shape[1]
    kern = functools.partial(_ffn_kernel, tiles_per_seq=seq // TM)
    return pl.pallas_call(
        kern,
        grid=(m // TM, n // SUB),
        in_specs=[
            pl.BlockSpec((TM, d), lambda i, j: (i, 0)),
            pl.BlockSpec((1, d), lambda i, j: (0, 0)),
            pl.BlockSpec((d, SUB), lambda i, j: (0, j)),
            pl.BlockSpec((d, SUB), lambda i, j: (0, j)),
            pl.BlockSpec((cw.shape[0], SUB), lambda i, j: (0, j)),
            pl.BlockSpec((SUB, d), lambda i, j: (j, 0)),
        ],
        out_specs=pl.BlockSpec((TM, d), lambda i, j: (i, 0)),
        out_shape=jax.ShapeDtypeStruct((m, d), F32),
        scratch_shapes=[
            pltpu.VMEM((TM, d), BF16),
            pltpu.VMEM((n // SUB, SUBLANES, SUB), F32),
            pltpu.VMEM((TM + SUBLANES, SUB), F32),
        ],
        compiler_params=_compiler_params(("arbitrary", "arbitrary"), 56 * 1024 * 1024),
        name="conv_ffn",
    )(x, gain, wu, wg, cw, wd)


def _gdn_proj_kernel(x_ref, g_ref, w_ref, cw_ref, o_ref, h_ref, tail_ref, ext_ref,
                     *, tiles_per_seq, n_head_steps, n_conv_steps, head_dim):
    i = pl.program_id(0)
    j = pl.program_id(1)

    @pl.when(j == 0)
    def _():
        h_ref[...] = _rms_rows(x_ref[...], g_ref[...]).astype(BF16)

    n_pieces = o_ref.shape[1] // SUB
    first = (i % tiles_per_seq) == 0

    def conv_piece(t):
        cols = slice(t * SUB, (t + 1) * SUB)
        y = _dot(h_ref[...], w_ref[:, cols])
        return _silu(_causal_conv_tile(y, cw_ref[:, cols], tail_ref, ext_ref.at[t],
                                       j * n_pieces + t, first))

    @pl.when(j < 2 * n_head_steps)
    def _():
        scale = jnp.where(j < n_head_steps, head_dim ** -0.5, 1.0)
        for t in range(n_pieces):
            _head_norm_store(conv_piece(t), o_ref, t * SUB, head_dim, scale, mean=False)

    @pl.when((j >= 2 * n_head_steps) & (j < n_conv_steps))
    def _():
        for t in range(n_pieces):
            o_ref[:, t * SUB:(t + 1) * SUB] = conv_piece(t).astype(o_ref.dtype)

    @pl.when(j >= n_conv_steps)
    def _():
        for t in range(n_pieces):
            cols = slice(t * SUB, (t + 1) * SUB)
            o_ref[:, cols] = _dot(h_ref[...], w_ref[:, cols]).astype(o_ref.dtype)


def _gdn_proj(x, gain, w, cw, seq, key_dim, head_dim):
    m, d = x.shape
    n = w.shape[1]
    n_conv_steps = cw.shape[1] // TN
    n_pieces = TN // SUB
    kern = functools.partial(
        _gdn_proj_kernel, tiles_per_seq=seq // TM, n_head_steps=key_dim // TN,
        n_conv_steps=n_conv_steps, head_dim=head_dim)
    return pl.pallas_call(
        kern,
        grid=(m // TM, n // TN),
        in_specs=[
            pl.BlockSpec((TM, d), lambda i, j: (i, 0)),
            pl.BlockSpec((1, d), lambda i, j: (0, 0)),
            pl.BlockSpec((d, TN), lambda i, j: (0, j)),
            pl.BlockSpec((cw.shape[0], TN), lambda i, j: (0, jnp.minimum(j, n_conv_steps - 1))),
        ],
        out_specs=pl.BlockSpec((TM, TN), lambda i, j: (i, j)),
        out_shape=jax.ShapeDtypeStruct((m, n), BF16),
        scratch_shapes=[
            pltpu.VMEM((TM, d), BF16),
            pltpu.VMEM((n_conv_steps * n_pieces, SUBLANES, SUB), F32),
            pltpu.VMEM((n_pieces, TM + SUBLANES, SUB), F32),
        ],
        compiler_params=_compiler_params(("arbitrary", "arbitrary"), 58 * 1024 * 1024),
        name="gdn_proj",
    )(x, gain, w, cw)


def _gdn_gate_kernel(x_ref, g_ref, w_ref, alog_ref, dtb_ref, o_ref, *, n_heads):
    h = _rms_rows(x_ref[...], g_ref[...]).astype(BF16)
    y = _dot(h, w_ref[...])
    is_decay = lax.broadcasted_iota(jnp.int32, (1, y.shape[1]), 1) < n_heads
    g = -jnp.exp(alog_ref[...]) * jax.nn.softplus(y + dtb_ref[...])
    vals = jnp.where(is_decay, g, jax.nn.sigmoid(y))
    ri = lax.broadcasted_iota(jnp.int32, (CHUNK, CHUNK), 0)
    ci = lax.broadcasted_iota(jnp.int32, (CHUNK, CHUNK), 1)
    tri = (ri >= ci).astype(F32)
    for c in range(y.shape[0] // CHUNK):
        rows = slice(c * CHUNK, (c + 1) * CHUNK)
        v = vals[rows, :]
        cum = jnp.dot(tri, v, precision=lax.Precision.HIGHEST, preferred_element_type=F32)
        o_ref[rows, :] = jnp.where(is_decay, cum, v)


def _gdn_gates(x, gain, w_ab, alog, dtb, n_heads):
    m, d = x.shape
    kern = functools.partial(_gdn_gate_kernel, n_heads=n_heads)
    return pl.pallas_call(
        kern,
        grid=(m // TM,),
        in_specs=[
            pl.BlockSpec((TM, d), lambda i: (i, 0)),
            pl.BlockSpec((1, d), lambda i: (0, 0)),
            pl.BlockSpec((d, LANES), lambda i: (0, 0)),
            pl.BlockSpec((1, LANES), lambda i: (0, 0)),
            pl.BlockSpec((1, LANES), lambda i: (0, 0)),
        ],
        out_specs=pl.BlockSpec((TM, LANES), lambda i: (i, 0)),
        out_shape=jax.ShapeDtypeStruct((m, LANES), F32),
        compiler_params=_compiler_params(("arbitrary",), 32 * 1024 * 1024),
        name="gdn_gates",
    )(x, gain, w_ab, alog, dtb)


def _mm_resid_kernel(a_ref, w_ref, r_ref, o_ref):
    o_ref[...] = r_ref[...] + _dot(a_ref[...], w_ref[...])


def _mm_resid(a, w, resid):
    m, k = a.shape
    n = w.shape[1]
    tm = TM // 2
    return pl.pallas_call(
        _mm_resid_kernel,
        grid=(m // tm,),
        in_specs=[
            pl.BlockSpec((tm, k), lambda i: (i, 0)),
            pl.BlockSpec((k, n), lambda i: (0, 0), pipeline_mode=pl.Buffered(1)),
            pl.BlockSpec((tm, n), lambda i: (i, 0)),
        ],
        out_specs=pl.BlockSpec((tm, n), lambda i: (i, 0)),
        out_shape=jax.ShapeDtypeStruct((m, n), F32),
        compiler_params=_compiler_params(("arbitrary",), 48 * 1024 * 1024),
        name="mm_resid",
    )(a, w, resid)


def _bias_table_kernel(rb_ref, idx_ref, add_ref, o_ref):
    nrel = rb_ref.shape[1]
    r = lax.broadcasted_iota(jnp.int32, (nrel, idx_ref.shape[1]), 0)
    onehot = (r == idx_ref[...]).astype(F32)
    o_ref[...] = jnp.dot(rb_ref[...], onehot, precision=lax.Precision.HIGHEST,
                         preferred_element_type=F32) * LOG2E + add_ref[...]


def _band_layout(rows, keys):
    qpos = np.arange(rows)[:, None]
    kpos = np.arange(keys)[None, :] - (keys - rows)
    qc = qpos // CHUNK
    kc = np.floor_divide(kpos, CHUNK)
    in_band = (kc <= qc) & (kc >= qc - LEFT_CHUNKS)
    idx = np.clip(qpos - kpos, -(CHUNK - 1), MAX_REL) + (CHUNK - 1)
    idx = np.where(in_band, idx, 0).astype(np.int32)
    add = np.where(in_band, 0.0, MASK_VALUE).astype(np.float32)
    return idx.reshape(1, -1), add.reshape(1, -1)


def _bias_table(rel_bias, rows, keys):
    heads, nrel = rel_bias.shape
    nrel_pad = -(-nrel // LANES) * LANES
    rb = jnp.pad(rel_bias.astype(F32), ((0, 0), (0, nrel_pad - nrel)))
    idx, add = _band_layout(rows, keys)
    n = rows * keys
    tb = 4096
    out = pl.pallas_call(
        _bias_table_kernel,
        grid=(n // tb,),
        in_specs=[
            pl.BlockSpec((heads, nrel_pad), lambda i: (0, 0)),
            pl.BlockSpec((1, tb), lambda i: (0, i)),
            pl.BlockSpec((1, tb), lambda i: (0, i)),
        ],
        out_specs=pl.BlockSpec((heads, tb), lambda i: (0, i)),
        out_shape=jax.ShapeDtypeStruct((heads, n), F32),
        compiler_params=_compiler_params(("arbitrary",), 32 * 1024 * 1024),
        name="att_bias_table",
    )(rb, jnp.asarray(idx), jnp.asarray(add))
    return out.reshape(heads, rows, keys)


def _attn_kernel(q_ref, k0_ref, k1_ref, k2_ref, v0_ref, v1_ref, v2_ref, bias_ref, o_ref,
                 *, head_dim):
    qb = pl.program_id(2)
    rows = q_ref.shape[0]
    k_refs = (k0_ref, k1_ref, k2_ref)
    v_refs = (v0_ref, v1_ref, v2_ref)
    nblk = len(k_refs)
    heads = range(q_ref.shape[1] // head_dim)
    sl = [slice(hh * head_dim, (hh + 1) * head_dim) for hh in heads]
    def qk(hh):
        q = q_ref[:, sl[hh]]
        row = []
        for t in range(nblk):
            s = _dot_nt(q, k_refs[t][:, sl[hh]]) + bias_ref[hh, :, t * rows:(t + 1) * rows]
            if t < nblk - 1:
                s = jnp.where(qb >= nblk - 1 - t, s, MASK_VALUE)
            row.append(s)
        return row

    def fold(parts, op):
        m = functools.reduce(op, parts)
        return functools.reduce(op, [m[:, i:i + LANES] for i in range(0, m.shape[1], LANES)])

    ones = jnp.ones((rows, head_dim), BF16)

    def softmax(sc):
        mx = fold(sc, jnp.maximum).max(axis=-1, keepdims=True)
        return [jnp.exp2((s - mx).astype(BF16)) for s in sc]

    def pv(hh, p):
        acc = sum(_dot(p[t], jnp.concatenate([v_refs[t][:, sl[hh]], ones], axis=1))
                  for t in range(nblk))
        o_ref[:, sl[hh]] = (acc[:, :head_dim] / acc[:, head_dim:]).astype(o_ref.dtype)

    sc = qk(0)
    for hh in heads:
        nxt = qk(hh + 1) if hh + 1 < len(heads) else None
        pv(hh, softmax(sc))
        sc = nxt


def _band_attention(qkv, bias, layer, batch, seq, d, head_dim):
    m = qkv.shape[0]
    r = ATT_ROWS
    gw = ATT_HEAD_GROUP * head_dim
    n_groups = d // gw
    nqb = seq // r
    nblk = bias.shape[2] // r

    def q_map(g, b, qb):
        return (b * nqb + qb, g)

    def kv_map(off, col0):
        def f(g, b, qb):
            return (b * nqb + jnp.maximum(qb - off, 0), col0 + g)
        return f

    k_specs = [pl.BlockSpec((r, gw), kv_map(nblk - 1 - t, n_groups)) for t in range(nblk)]
    v_specs = [pl.BlockSpec((r, gw), kv_map(nblk - 1 - t, 2 * n_groups)) for t in range(nblk)]
    kern = functools.partial(_attn_kernel, head_dim=head_dim)
    return pl.pallas_call(
        kern,
        grid=(n_groups, batch, nqb),
        in_specs=[pl.BlockSpec((r, gw), q_map)] + k_specs + v_specs + [
            pl.BlockSpec((ATT_HEAD_GROUP, r, nblk * r),
                         lambda g, b, qb: (layer * n_groups + g, 0, 0),
                         pipeline_mode=pl.Buffered(1))],
        out_specs=pl.BlockSpec((r, gw), q_map),
        out_shape=jax.ShapeDtypeStruct((m, d), BF16),
        compiler_params=_compiler_params(("arbitrary", "arbitrary", "arbitrary"),
                                         48 * 1024 * 1024),
        name="band_attention",
    )(qkv, qkv, qkv, qkv, qkv, qkv, qkv, bias)


def _pool_kernel(x_ref, g_ref, w_ref, sc_ref, o_ref, ext_ref, *, tiles_per_seq, windows):
    i = pl.program_id(0)
    tm, d = x_ref.shape
    halo = max(windows)
    x = x_ref[...]
    h = _rms_rows(x, g_ref[...])
    first = (i % tiles_per_seq) == 0

    @pl.when(first)
    def _():
        ext_ref[0:halo, :] = jnp.zeros((halo, d), F32)

    @pl.when(jnp.logical_not(first))
    def _():
        ext_ref[0:halo, :] = ext_ref[tm:tm + halo, :]

    ext_ref[halo:halo + tm, :] = h
    pos = (i % tiles_per_seq) * tm + lax.broadcasted_iota(jnp.int32, (tm, 1), 0)
    dg = d // len(windows)
    for g, w in enumerate(windows):
        cs = slice(g * dg, (g + 1) * dg)
        hg = h[:, cs]
        acc = hg
        for s in range(1, w):
            acc = acc + ext_ref[pl.ds(halo - s, tm), cs]
        cnt = jnp.minimum(pos + 1, w).astype(F32)
        pooled = acc / cnt - hg
        y = _dot(pooled.astype(BF16), w_ref[g]) * sc_ref[:, cs]
        o_ref[:, cs] = x[:, cs] + y


def _pool_mixer(x, gain, pool_w, pool_scale, seq):
    m, d = x.shape
    g, dg, _ = pool_w.shape
    halo = max(POOL_WINDOWS)
    kern = functools.partial(_pool_kernel, tiles_per_seq=seq // POOL_TM, windows=POOL_WINDOWS)
    return pl.pallas_call(
        kern,
        grid=(m // POOL_TM,),
        in_specs=[
            pl.BlockSpec((POOL_TM, d), lambda i: (i, 0)),
            pl.BlockSpec((1, d), lambda i: (0, 0)),
            pl.BlockSpec((g, dg, dg), lambda i: (0, 0, 0)),
            pl.BlockSpec((1, d), lambda i: (0, 0)),
        ],
        out_specs=pl.BlockSpec((POOL_TM, d), lambda i: (i, 0)),
        out_shape=jax.ShapeDtypeStruct((m, d), F32),
        scratch_shapes=[pltpu.VMEM((POOL_TM + halo, d), F32)],
        compiler_params=_compiler_params(("arbitrary",), 40 * 1024 * 1024),
        name="pool_mixer",
    )(x, gain, pool_w, pool_scale)


def _mm_inv(p, q):
    return _dot(p.astype(BF16), q.astype(BF16))


def _gdn_kernel(q_ref, k_ref, v_ref, gb_ref, rows_ref, gate_ref, srows_ref, og_ref, o_ref,
                s_ref, gq_ref, ho_ref, *, n_steps, n_kheads, n_blocks, n_chunks, n_heads, dv):
    t = pl.program_id(0)
    slot_p = t % 2
    slot_s = 1 - slot_p
    hk = (jnp.minimum(t, n_steps - 1) // n_blocks) % n_kheads
    scan_block = jnp.maximum(t - 1, 0) % n_blocks

    c2 = 2 * CHUNK
    blk = c2 + CHUNK
    ri = lax.broadcasted_iota(jnp.int32, (c2, c2), 0)
    ci = lax.broadcasted_iota(jnp.int32, (c2, c2), 1)
    same = (ri // CHUNK) == (ci // CHUNK)
    causal = same & (ri >= ci)
    strict = same & (ri > ci)
    eye = (ri == ci).astype(F32)
    head0_lanes = ci < CHUNK
    lane = lax.broadcasted_iota(jnp.int32, (CHUNK, LANES), 1)
    og = og_ref[...]
    n = range(n_chunks)

    def col(gb, idx):
        return jnp.sum(jnp.where(lane == idx, gb, 0.0), axis=1, keepdims=True)

    def prepare():
        r0 = [c * CHUNK for c in n]
        q = [q_ref[r:r + CHUNK, :] for r in r0]
        k = [k_ref[r:r + CHUNK, :] for r in r0]
        gb = [gb_ref[r:r + CHUNK, :] for r in r0]
        rows = [rows_ref[0, c] for c in n]
        grow = [x[0:1] for x in rows]
        brow = [x[1:2] for x in rows]
        glast = [x[2:3] for x in rows]
        gcol = [jnp.concatenate([col(g, 2 * hk), col(g, 2 * hk + 1)], axis=0) for g in gb]
        bcol = [jnp.concatenate([col(g, n_heads + 2 * hk), col(g, n_heads + 2 * hk + 1)], axis=0)
                for g in gb]
        yield
        k2 = [jnp.concatenate([x, x], axis=0) for x in k]
        qkk = [_dot_nt(jnp.concatenate([q[i], k[i]], axis=0), k2[i]) for i in n]
        yield
        qk = [jnp.concatenate([x[:CHUNK], x[:CHUNK]], axis=0) for x in qkk]
        kk = [jnp.concatenate([x[CHUNK:], x[CHUNK:]], axis=0) for x in qkk]
        decay = [jnp.where(causal, jnp.exp(jnp.where(causal, gcol[i] - grow[i], 0.0)), 0.0)
                 for i in n]
        a = [jnp.where(strict, bcol[i] * kk[i] * decay[i], 0.0) for i in n]
        attn = [qk[i] * decay[i] for i in n]
        yield
        base = SUBLANES
        bmask = (ri // base) == (ci // base)
        a0 = [jnp.where(bmask, m, 0.0) for m in a]
        p = [_mm_inv(m, m) for m in a0]
        x = [eye - m for m in a0]
        yield
        x = [xi + _mm_inv(xi, pi) for xi, pi in zip(x, p)]
        yield
        p = [_mm_inv(pi, pi) for pi in p]
        yield
        x = [xi + _mm_inv(xi, pi) for xi, pi in zip(x, p)]
        yield
        s = base
        while s < CHUNK:
            nmask = ((ri // (2 * s)) == (ci // (2 * s))) & ((ri // s) != (ci // s))
            y = [_mm_inv(jnp.where(nmask, ai, 0.0), xi) for ai, xi in zip(a, x)]
            yield
            x = [xi - _mm_inv(xi, yi) for xi, yi in zip(x, y)]
            yield
            s *= 2
        v = [v_ref[r:r + CHUNK, :] for r in r0]
        v2 = [jnp.concatenate([m[:, :dv], m[:, dv:]], axis=0) for m in v]
        u = [_dot((x[i] * brow[i]).astype(BF16), v2[i]) for i in n]
        yield
        w = [_dot((x[i] * (brow[i] * jnp.exp(grow[i]))).astype(BF16), k2[i]) for i in n]
        yield
        wu = [jnp.concatenate([w[i], u[i]], axis=1).astype(BF16) for i in n]
        kte = [k2[i].astype(F32).T * jnp.exp(glast[i] - grow[i]) for i in n]
        lhs = [jnp.concatenate([
            jnp.where(head0_lanes, kte[i], 0.0),
            jnp.where(head0_lanes, 0.0, kte[i]),
            attn[i]], axis=0).astype(BF16) for i in n]
        yield
        r = [_dot(lhs[i], wu[i]) for i in n]
        yield
        for i in n:
            q2f = jnp.concatenate([q[i], q[i]], axis=0).astype(F32)
            qmat = q2f * jnp.exp(gcol[i]) - r[i][2 * c2:, :dv]
            omat = r[i][2 * c2:, dv:]
            gq_ref[slot_p, i] = jnp.concatenate([
                -r[i][:c2, :dv], qmat[:CHUNK], -r[i][c2:2 * c2, :dv], qmat[CHUNK:]],
                axis=0).astype(BF16)
            ho_ref[slot_p, i] = jnp.concatenate([
                r[i][:c2, dv:], omat[:CHUNK], r[i][c2:2 * c2, dv:], omat[CHUNK:]], axis=0)
        yield

    def scan():
        fresh = scan_block == 0
        states = [jnp.where(fresh, 0.0, s_ref[j]) for j in range(2)]
        for c in n:
            rows = srows_ref[0, c]
            gate = gate_ref[c * CHUNK:(c + 1) * CHUNK, :]
            outs = []
            for j in range(2):
                s = states[j]
                r = _dot(gq_ref[slot_s, c, j * blk:(j + 1) * blk, :], s.astype(BF16))
                ho = ho_ref[slot_s, c, j * blk:(j + 1) * blk, :]
                dec = jnp.exp(rows[3 + j:4 + j])
                states[j] = s * dec + r[:c2] + ho[:c2]
                o = r[c2:] + ho[c2:]
                on = o * lax.rsqrt(jnp.mean(o * o, axis=-1, keepdims=True) + EPS) * og
                outs.append(on * _silu(gate[:, j * dv:(j + 1) * dv].astype(F32)))
            o_ref[c * CHUNK:(c + 1) * CHUNK, :] = jnp.concatenate(outs, axis=1).astype(o_ref.dtype)
            yield
        s_ref[0] = states[0]
        s_ref[1] = states[1]
        yield

    def run(generators, weights):
        live = list(generators)
        while any(g is not None for g in live):
            for i, g in enumerate(live):
                for _ in range(weights[i]):
                    if live[i] is not None and next(live[i], "done") == "done":
                        live[i] = None

    @pl.when(t == 0)
    def _():
        s_ref[...] = jnp.zeros(s_ref.shape, F32)
        run([prepare()], [1])

    @pl.when((t > 0) & (t < n_steps))
    def _():
        run([prepare(), scan()], [2, 1])

    @pl.when(t == n_steps)
    def _():
        run([scan()], [1])


def _gdn_rows(gb, n_heads, n_kheads):
    m = gb.shape[0]
    nc = m // CHUNK
    rep = n_heads // n_kheads
    assert rep * CHUNK == LANES

    def stack(x):
        x = x.reshape(nc, CHUNK, n_kheads, rep).transpose(2, 0, 3, 1)
        return x.reshape(n_kheads, nc, rep * CHUNK)

    gc = gb[:, :n_heads].reshape(nc, CHUNK, n_heads)
    beta = gb[:, n_heads:2 * n_heads].reshape(nc, CHUNK, n_heads)
    last = jnp.broadcast_to(gc[:, CHUNK - 1:, :], gc.shape)
    last_k = gc[:, CHUNK - 1, :].reshape(nc, n_kheads, rep).transpose(1, 0, 2)
    per_head = [jnp.broadcast_to(last_k[:, :, r:r + 1], (n_kheads, nc, LANES)) for r in range(rep)]
    rows = [stack(gc), stack(beta), stack(last)] + per_head
    rows += [jnp.zeros_like(rows[0])] * (SUBLANES - len(rows))
    return jnp.stack(rows, axis=2)


def _gated_delta(proj, gb, rows, o_gain, batch, seq, n_kheads, n_heads, dk, dv):
    m = proj.shape[0]
    cb = GDN_CHUNKS_PER_STEP
    rb = cb * CHUNK
    ncb = seq // rb
    key_dim = n_kheads * dk
    val_dim = n_heads * dv
    vw = 2 * dv
    n_steps = batch * n_kheads * ncb
    assert n_heads == 2 * n_kheads and dk == LANES and dv == LANES

    def decode(idx):
        return (idx // (n_kheads * ncb)) * ncb + idx % ncb, (idx // ncb) % n_kheads

    def prep(col0):
        def f(t):
            row, hk = decode(jnp.minimum(t, n_steps - 1))
            return row, col0 + hk
        return f

    def lagged(col0):
        def f(t):
            row, hk = decode(jnp.maximum(t - 1, 0))
            return row, col0 + hk
        return f

    def rows_map(f):
        def g(t):
            row, hk = f(t)
            return hk, row, 0, 0
        return g

    kern = functools.partial(_gdn_kernel, n_steps=n_steps, n_kheads=n_kheads, n_blocks=ncb,
                             n_chunks=cb, n_heads=n_heads, dv=dv)
    return pl.pallas_call(
        kern,
        grid=(n_steps + 1,),
        in_specs=[
            pl.BlockSpec((rb, dk), prep(0)),
            pl.BlockSpec((rb, dk), prep(key_dim // dk)),
            pl.BlockSpec((rb, vw), prep(2 * key_dim // vw)),
            pl.BlockSpec((rb, LANES), lambda t: (prep(0)(t)[0], 0)),
            pl.BlockSpec((1, cb, SUBLANES, LANES), rows_map(prep(0))),
            pl.BlockSpec((rb, vw), lagged((2 * key_dim + val_dim) // vw)),
            pl.BlockSpec((1, cb, SUBLANES, LANES), rows_map(lagged(0))),
            pl.BlockSpec((1, dv), lambda t: (0, 0)),
        ],
        out_specs=pl.BlockSpec((rb, vw), lagged(0)),
        out_shape=jax.ShapeDtypeStruct((m, val_dim), BF16),
        scratch_shapes=[
            pltpu.VMEM((2, dk, dv), F32),
            pltpu.VMEM((2, cb, 2 * (dk + CHUNK), dv), BF16),
            pltpu.VMEM((2, cb, 2 * (dk + CHUNK), dv), F32),
        ],
        compiler_params=_compiler_params(("arbitrary",), 32 * 1024 * 1024),
        name="gated_delta",
    )(proj, proj, proj, gb, rows, proj, rows, o_gain)


def _pad_cols(w, n):
    return jnp.pad(w, ((0, 0), (0, n - w.shape[1])))


def _attention_layer(x, gain, w_qkv, q_gain, k_gain, bias, layer, w_o, batch, seq):
    d = x.shape[1]
    head_dim = d // ATT_HEADS
    qk_gain = jnp.zeros((SUBLANES, head_dim), F32)
    qk_gain = qk_gain.at[0].set(q_gain.astype(F32) * (head_dim ** -0.5 * LOG2E)).at[1].set(k_gain.astype(F32))
    qkv = _qkv_proj(x, gain, w_qkv.astype(BF16), qk_gain, head_dim)
    o = _band_attention(qkv, bias, layer, batch, seq, d, head_dim)
    return _mm_resid(o, w_o.astype(BF16), x)


def _gdn_layer(x, gain, w_in, conv_w, a_log, dt_bias, o_gain, w_o, batch, seq):
    d = x.shape[1]
    dk = d // GDN_K_HEADS
    dv = dk
    key_dim = GDN_K_HEADS * dk
    val_dim = GDN_V_HEADS * dv
    conv_ch = 2 * key_dim + val_dim
    n_main = conv_ch + val_dim
    proj = _gdn_proj(x, gain, w_in[:, :n_main].astype(BF16), conv_w.astype(F32), seq, key_dim, dk)
    w_ab = _pad_cols(w_in[:, n_main:], LANES).astype(BF16)
    pad = LANES - GDN_V_HEADS
    alog = jnp.pad(a_log.astype(F32), (0, pad)).reshape(1, LANES)
    dtb = jnp.pad(dt_bias.astype(F32), (0, pad)).reshape(1, LANES)
    gb = _gdn_gates(x, gain, w_ab, alog, dtb, GDN_V_HEADS)
    rows = _gdn_rows(gb, GDN_V_HEADS, GDN_K_HEADS)
    o = _gated_delta(proj, gb, rows, o_gain.astype(F32).reshape(1, dv), batch, seq,
                     GDN_K_HEADS, GDN_V_HEADS, dk, dv)
    return _mm_resid(o, w_o.astype(BF16), x)


def _conv_ffn(x, gain, w_up, conv_w, w_down, seq):
    d_ff = conv_w.shape[1]
    d_ff_pad = -(-d_ff // SUB) * SUB
    wu = _pad_cols(w_up[:, :d_ff], d_ff_pad).astype(BF16)
    wg = _pad_cols(w_up[:, d_ff:], d_ff_pad).astype(BF16)
    cw = _pad_cols(conv_w.astype(F32), d_ff_pad)
    wd = jnp.pad(w_down, ((0, d_ff_pad - d_ff), (0, 0))).astype(BF16)
    return _conv_ffn_call(x, gain, wu, wg, cw, wd, seq)


def kernel(x, mix_norm, ffn_norm, att_w_qkv, att_q_gain, att_k_gain, att_rel_bias, att_w_o,
           pool_w, pool_scale, gdn_w_in, gdn_conv, gdn_a_log, gdn_dt_bias, gdn_o_gain,
           gdn_w_o, ffn_w_up, ffn_conv, ffn_w_down):
    batch, seq, d = x.shape
    depth = mix_norm.shape[0]
    assert seq % TM == 0 and seq % POOL_TM == 0 and seq % ATT_ROWS == 0
    assert seq % (GDN_CHUNKS_PER_STEP * CHUNK) == 0
    xf = x.reshape(batch * seq, d).astype(F32)
    att_bias = _bias_table(att_rel_bias.reshape(-1, att_rel_bias.shape[-1]), ATT_ROWS,
                           ATT_ROWS + LEFT_CHUNKS * CHUNK)
    for i in range(depth):
        kind = i % N_MIXERS
        j = i // N_MIXERS
        gain = mix_norm[i].astype(F32).reshape(1, d)
        if kind == 0:
            xf = _attention_layer(xf, gain, att_w_qkv[j], att_q_gain[j], att_k_gain[j],
                                  att_bias, j, att_w_o[j], batch, seq)
        elif kind == 1:
            xf = _pool_mixer(xf, gain, pool_w[j].astype(BF16),
                             pool_scale[j].astype(F32).reshape(1, d), seq)
        else:
            xf = _gdn_layer(xf, gain, gdn_w_in[j], gdn_conv[j], gdn_a_log[j], gdn_dt_bias[j],
                            gdn_o_gain[j], gdn_w_o[j], batch, seq)
        fgain = ffn_norm[i].astype(F32).reshape(1, d)
        xf = _conv_ffn(xf, fgain, ffn_w_up[i], ffn_conv[i], ffn_w_down[i], seq)
    return xf.reshape(batch, seq, d).astype(x.dtype)
```

```python
import functools

import numpy as np
import jax
import jax.numpy as jnp
from jax import lax
from jax.experimental import pallas as pl
from jax.experimental.pallas import tpu as pltpu

F32 = jnp.float32
BF16 = jnp.bfloat16

CHUNK = 64
EPS = 1e-6
MASK_VALUE = -1e30
LOG2E = 1.4426950408889634
N_MIXERS = 3

ATT_HEADS = 16
LEFT_CHUNKS = 8
MAX_REL = 256

POOL_WINDOWS = (2, 4, 8, 16)

GDN_K_HEADS = 16
GDN_V_HEADS = 32

LANES = 128
SUBLANES = 8
V7X_VMEM_BYTES = 64 * 1024 * 1024

TM = 1024
TN = 2048
SUB = 512
POOL_TM = 512
POOL_HALO = 32
ATT_ROWS = 256
ATT_HEAD_GROUP = 16
GDN_CHUNKS_PER_STEP = 16


def _compiler_params(semantics, vmem_bytes):
    assert vmem_bytes < V7X_VMEM_BYTES
    return pltpu.CompilerParams(dimension_semantics=semantics, vmem_limit_bytes=vmem_bytes)


def _rms_rows(x, gain):
    ms = jnp.mean(x * x, axis=-1, keepdims=True)
    return x * lax.rsqrt(ms + EPS) * gain


def _silu(x):
    return x * jax.nn.sigmoid(x)


def _dot(a, b):
    return jnp.dot(a, b, preferred_element_type=F32)


def _dot_nt(a, b):
    return lax.dot_general(a, b, (((1,), (1,)), ((), ())), preferred_element_type=F32)


def _causal_conv_tile(u, cw, tail_ref, ext_ref, t, first):
    tm = u.shape[0]
    k = cw.shape[0]
    prev = jnp.where(first, 0.0, tail_ref[t])
    ext_ref[0:SUBLANES, :] = prev
    ext_ref[SUBLANES:SUBLANES + tm, :] = u
    tail_ref[t] = u[tm - SUBLANES:, :]
    acc = u * cw[k - 1:k, :]
    for s in range(1, k):
        acc = acc + ext_ref[pl.ds(SUBLANES - s, tm), :] * cw[k - 1 - s:k - s, :]
    return acc


def _head_norm_store(y, o_ref, col0, head_dim, gain, mean):
    for hh in range(y.shape[1] // head_dim):
        sl = slice(hh * head_dim, (hh + 1) * head_dim)
        yh = y[:, sl]
        ss = jnp.sum(yh * yh, axis=-1, keepdims=True)
        r = lax.rsqrt((ss / head_dim if mean else ss) + EPS)
        o_ref[:, col0 + hh * head_dim:col0 + (hh + 1) * head_dim] = (yh * r * gain).astype(o_ref.dtype)


def _qkv_kernel(x_ref, g_ref, w_ref, qkg_ref, o_ref, h_ref, *, steps_per_kind, head_dim):
    j = pl.program_id(1)

    @pl.when(j == 0)
    def _():
        h_ref[...] = _rms_rows(x_ref[...], g_ref[...]).astype(BF16)

    kind = j // steps_per_kind
    pieces = range(o_ref.shape[1] // SUB)

    @pl.when(kind < 2)
    def _():
        gain = qkg_ref[pl.ds(kind, 1), :]
        for t in pieces:
            y = _dot(h_ref[...], w_ref[:, t * SUB:(t + 1) * SUB])
            _head_norm_store(y, o_ref, t * SUB, head_dim, gain, mean=True)

    @pl.when(kind == 2)
    def _():
        for t in pieces:
            cols = slice(t * SUB, (t + 1) * SUB)
            o_ref[:, cols] = _dot(h_ref[...], w_ref[:, cols]).astype(o_ref.dtype)


def _qkv_proj(x, gain, w, qk_gain, head_dim):
    m, d = x.shape
    n = w.shape[1]
    kern = functools.partial(_qkv_kernel, steps_per_kind=d // TN, head_dim=head_dim)
    return pl.pallas_call(
        kern,
        grid=(m // TM, n // TN),
        in_specs=[
            pl.BlockSpec((TM, d), lambda i, j: (i, 0)),
            pl.BlockSpec((1, d), lambda i, j: (0, 0)),
            pl.BlockSpec((d, TN), lambda i, j: (0, j)),
            pl.BlockSpec((SUBLANES, head_dim), lambda i, j: (0, 0)),
        ],
        out_specs=pl.BlockSpec((TM, TN), lambda i, j: (i, j)),
        out_shape=jax.ShapeDtypeStruct((m, n), BF16),
        scratch_shapes=[pltpu.VMEM((TM, d), BF16)],
        compiler_params=_compiler_params(("arbitrary", "arbitrary"), 56 * 1024 * 1024),
        name="qkv_proj",
    )(x, gain, w, qk_gain)


def _ffn_kernel(x_ref, g_ref, wu_ref, wg_ref, cw_ref, wd_ref, o_ref, h_ref, tail_ref, ext_ref,
                *, tiles_per_seq):
    i = pl.program_id(0)
    j = pl.program_id(1)

    @pl.when(j == 0)
    def _():
        x = x_ref[...]
        h_ref[...] = _rms_rows(x, g_ref[...]).astype(BF16)
        o_ref[...] = x

    h = h_ref[...]
    u = _dot(h, wu_ref[...])
    gate = _dot(h, wg_ref[...])
    first = (i % tiles_per_seq) == 0
    c = _causal_conv_tile(u, cw_ref[...], tail_ref, ext_ref, j, first)
    act = (_silu(c) * gate).astype(BF16)
    o_ref[...] += _dot(act, wd_ref[...])


def _conv_ffn_call(x, gain, wu, wg, cw, wd, seq):
    m, d = x.shape
    n = wu.shape[1]
    kern = functools.partial(_ffn_kernel, tiles_per_seq=seq // TM)
    return pl.pallas_call(
        kern,
        grid=(m // TM, n // SUB),
        in_specs=[
            pl.BlockSpec((TM, d), lambda i, j: (i, 0)),
            pl.BlockSpec((1, d), lambda i, j: (0, 0)),
            pl.BlockSpec((d, SUB), lambda i, j: (0, j)),
            pl.BlockSpec((d, SUB), lambda i, j: (0, j)),
            pl.BlockSpec((cw.shape[0], SUB), lambda i, j: (0, j)),
            pl.BlockSpec((SUB, d), lambda i, j: (j, 0)),
        ],
        out_specs=pl.BlockSpec((TM, d), lambda i, j: (i, 0)),
        out_shape=jax.ShapeDtypeStruct((m, d), F32),
        scratch_shapes=[
            pltpu.VMEM((TM, d), BF16),
            pltpu.VMEM((n // SUB, SUBLANES, SUB), F32),
            pltpu.VMEM((TM + SUBLANES, SUB), F32),
        ],
        compiler_params=_compiler_params(("arbitrary", "arbitrary"), 56 * 1024 * 1024),
        name="conv_ffn",
    )(x, gain, wu, wg, cw, wd)


def _gdn_proj_kernel(x_ref, g_ref, w_ref, cw_ref, o_ref, h_ref, tail_ref, ext_ref,
                     *, tiles_per_seq, n_head_steps, n_conv_steps, head_dim):
    i = pl.program_id(0)
    j = pl.program_id(1)

    @pl.when(j == 0)
    def _():
        h_ref[...] = _rms_rows(x_ref[...], g_ref[...]).astype(BF16)

    n_pieces = o_ref.shape[1] // SUB
    first = (i % tiles_per_seq) == 0

    def conv_piece(t):
        cols = slice(t * SUB, (t + 1) * SUB)
        y = _dot(h_ref[...], w_ref[:, cols])
        return _silu(_causal_conv_tile(y, cw_ref[:, cols], tail_ref, ext_ref.at[t],
                                       j * n_pieces + t, first))

    @pl.when(j < 2 * n_head_steps)
    def _():
        scale = jnp.where(j < n_head_steps, head_dim ** -0.5, 1.0)
        for t in range(n_pieces):
            _head_norm_store(conv_piece(t), o_ref, t * SUB, head_dim, scale, mean=False)

    @pl.when((j >= 2 * n_head_steps) & (j < n_conv_steps))
    def _():
        for t in range(n_pieces):
            o_ref[:, t * SUB:(t + 1) * SUB] = conv_piece(t).astype(o_ref.dtype)

    @pl.when(j >= n_conv_steps)
    def _():
        for t in range(n_pieces):
            cols = slice(t * SUB, (t + 1) * SUB)
            o_ref[:, cols] = _dot(h_ref[...], w_ref[:, cols]).astype(o_ref.dtype)


def _gdn_proj(x, gain, w, cw, seq, key_dim, head_dim):
    m, d = x.shape
    n = w.shape[1]
    n_conv_steps = cw.shape[1] // TN
    n_pieces = TN // SUB
    kern = functools.partial(
        _gdn_proj_kernel, tiles_per_seq=seq // TM, n_head_steps=key_dim // TN,
        n_conv_steps=n_conv_steps, head_dim=head_dim)
    return pl.pallas_call(
        kern,
        grid=(m // TM, n // TN),
        in_specs=[
            pl.BlockSpec((TM, d), lambda i, j: (i, 0)),
            pl.BlockSpec((1, d), lambda i, j: (0, 0)),
            pl.BlockSpec((d, TN), lambda i, j: (0, j)),
            pl.BlockSpec((cw.shape[0], TN), lambda i, j: (0, jnp.minimum(j, n_conv_steps - 1))),
        ],
        out_specs=pl.BlockSpec((TM, TN), lambda i, j: (i, j)),
        out_shape=jax.ShapeDtypeStruct((m, n), BF16),
        scratch_shapes=[
            pltpu.VMEM((TM, d), BF16),
            pltpu.VMEM((n_conv_steps * n_pieces, SUBLANES, SUB), F32),
            pltpu.VMEM((n_pieces, TM + SUBLANES, SUB), F32),
        ],
        compiler_params=_compiler_params(("arbitrary", "arbitrary"), 58 * 1024 * 1024),
        name="gdn_proj",
    )(x, gain, w, cw)


def _gdn_gate_kernel(x_ref, g_ref, w_ref, alog_ref, dtb_ref, o_ref, *, n_heads):
    h = _rms_rows(x_ref[...], g_ref[...]).astype(BF16)
    y = _dot(h, w_ref[...])
    is_decay = lax.broadcasted_iota(jnp.int32, (1, y.shape[1]), 1) < n_heads
    g = -jnp.exp(alog_ref[...]) * jax.nn.softplus(y + dtb_ref[...])
    vals = jnp.where(is_decay, g, jax.nn.sigmoid(y))
    ri = lax.broadcasted_iota(jnp.int32, (CHUNK, CHUNK), 0)
    ci = lax.broadcasted_iota(jnp.int32, (CHUNK, CHUNK), 1)
    tri = (ri >= ci).astype(F32)
    for c in range(y.shape[0] // CHUNK):
        rows = slice(c * CHUNK, (c + 1) * CHUNK)
        v = vals[rows, :]
        cum = jnp.dot(tri, v, precision=lax.Precision.HIGHEST, preferred_element_type=F32)
        o_ref[rows, :] = jnp.where(is_decay, cum, v)


def _gdn_gates(x, gain, w_ab, alog, dtb, n_heads):
    m, d = x.shape
    kern = functools.partial(_gdn_gate_kernel, n_heads=n_heads)
    return pl.pallas_call(
        kern,
        grid=(m // TM,),
        in_specs=[
            pl.BlockSpec((TM, d), lambda i: (i, 0)),
            pl.BlockSpec((1, d), lambda i: (0, 0)),
            pl.BlockSpec((d, LANES), lambda i: (0, 0)),
            pl.BlockSpec((1, LANES), lambda i: (0, 0)),
            pl.BlockSpec((1, LANES), lambda i: (0, 0)),
        ],
        out_specs=pl.BlockSpec((TM, LANES), lambda i: (i, 0)),
        out_shape=jax.ShapeDtypeStruct((m, LANES), F32),
        compiler_params=_compiler_params(("arbitrary",), 32 * 1024 * 1024),
        name="gdn_gates",
    )(x, gain, w_ab, alog, dtb)


def _mm_resid_kernel(a_ref, w_ref, r_ref, o_ref):
    o_ref[...] = r_ref[...] + _dot(a_ref[...], w_ref[...])


def _mm_resid(a, w, resid):
    m, k = a.shape
    n = w.shape[1]
    tm = TM // 2
    return pl.pallas_call(
        _mm_resid_kernel,
        grid=(m // tm,),
        in_specs=[
            pl.BlockSpec((tm, k), lambda i: (i, 0)),
            pl.BlockSpec((k, n), lambda i: (0, 0), pipeline_mode=pl.Buffered(1)),
            pl.BlockSpec((tm, n), lambda i: (i, 0)),
        ],
        out_specs=pl.BlockSpec((tm, n), lambda i: (i, 0)),
        out_shape=jax.ShapeDtypeStruct((m, n), F32),
        compiler_params=_compiler_params(("arbitrary",), 48 * 1024 * 1024),
        name="mm_resid",
    )(a, w, resid)


def _bias_table_kernel(rb_ref, idx_ref, add_ref, o_ref):
    nrel = rb_ref.shape[1]
    r = lax.broadcasted_iota(jnp.int32, (nrel, idx_ref.shape[1]), 0)
    onehot = jnp.where(r == idx_ref[...], 1.0, 0.0).astype(BF16)
    rb = rb_ref[...]
    hi = rb.astype(BF16)
    rest = rb - hi.astype(F32)
    mid = rest.astype(BF16)
    lo = (rest - mid.astype(F32)).astype(BF16)
    picked = _dot(hi, onehot) + _dot(mid, onehot) + _dot(lo, onehot)
    o_ref[...] = picked * LOG2E + add_ref[...]


def _band_layout(rows, keys):
    qpos = np.arange(rows)[:, None]
    kpos = np.arange(keys)[None, :] - (keys - rows)
    qc = qpos // CHUNK
    kc = np.floor_divide(kpos, CHUNK)
    in_band = (kc <= qc) & (kc >= qc - LEFT_CHUNKS)
    idx = np.clip(qpos - kpos, -(CHUNK - 1), MAX_REL) + (CHUNK - 1)
    idx = np.where(in_band, idx, 0).astype(np.int32)
    add = np.where(in_band, 0.0, MASK_VALUE).astype(np.float32)
    return idx.reshape(1, -1), add.reshape(1, -1)


def _bias_table(rel_bias, rows, keys):
    heads, nrel = rel_bias.shape
    nrel_pad = -(-nrel // LANES) * LANES
    rb = jnp.pad(rel_bias.astype(F32), ((0, 0), (0, nrel_pad - nrel)))
    idx, add = _band_layout(rows, keys)
    n = rows * keys
    tb = 4096
    out = pl.pallas_call(
        _bias_table_kernel,
        grid=(n // tb,),
        in_specs=[
            pl.BlockSpec((heads, nrel_pad), lambda i: (0, 0)),
            pl.BlockSpec((1, tb), lambda i: (0, i)),
            pl.BlockSpec((1, tb), lambda i: (0, i)),
        ],
        out_specs=pl.BlockSpec((heads, tb), lambda i: (0, i)),
        out_shape=jax.ShapeDtypeStruct((heads, n), F32),
        compiler_params=_compiler_params(("arbitrary",), 32 * 1024 * 1024),
        name="att_bias_table",
    )(rb, jnp.asarray(idx), jnp.asarray(add))
    return out.reshape(heads, rows, keys)


def _attn_kernel(q_ref, k0_ref, k1_ref, k2_ref, v0_ref, v1_ref, v2_ref, bias_ref, o_ref,
                 *, head_dim):
    qb = pl.program_id(2)
    rows = q_ref.shape[0]
    k_refs = (k0_ref, k1_ref, k2_ref)
    v_refs = (v0_ref, v1_ref, v2_ref)
    nblk = len(k_refs)
    heads = range(q_ref.shape[1] // head_dim)
    sl = [slice(hh * head_dim, (hh + 1) * head_dim) for hh in heads]
    def qk(hh):
        q = q_ref[:, sl[hh]]
        row = []
        for t in range(nblk):
            s = _dot_nt(q, k_refs[t][:, sl[hh]]) + bias_ref[hh, :, t * rows:(t + 1) * rows]
            if t < nblk - 1:
                s = jnp.where(qb >= nblk - 1 - t, s, MASK_VALUE)
            row.append(s)
        return row

    def fold(parts, op):
        m = functools.reduce(op, parts)
        return functools.reduce(op, [m[:, i:i + LANES] for i in range(0, m.shape[1], LANES)])

    ones = jnp.ones((rows, head_dim), BF16)

    def softmax(sc):
        mx = fold(sc, jnp.maximum).max(axis=-1, keepdims=True)
        return [jnp.exp2((s - mx).astype(BF16)) for s in sc]

    def pv(hh, p):
        acc = sum(_dot(p[t], jnp.concatenate([v_refs[t][:, sl[hh]], ones], axis=1))
                  for t in range(nblk))
        o_ref[:, sl[hh]] = (acc[:, :head_dim] / acc[:, head_dim:]).astype(o_ref.dtype)

    sc = qk(0)
    for hh in heads:
        nxt = qk(hh + 1) if hh + 1 < len(heads) else None
        pv(hh, softmax(sc))
        sc = nxt


def _band_attention(qkv, bias, layer, batch, seq, d, head_dim):
    m = qkv.shape[0]
    r = ATT_ROWS
    gw = ATT_HEAD_GROUP * head_dim
    n_groups = d // gw
    nqb = seq // r
    nblk = bias.shape[2] // r

    def q_map(g, b, qb):
        return (b * nqb + qb, g)

    def kv_map(off, col0):
        def f(g, b, qb):
            return (b * nqb + jnp.maximum(qb - off, 0), col0 + g)
        return f

    k_specs = [pl.BlockSpec((r, gw), kv_map(nblk - 1 - t, n_groups)) for t in range(nblk)]
    v_specs = [pl.BlockSpec((r, gw), kv_map(nblk - 1 - t, 2 * n_groups)) for t in range(nblk)]
    kern = functools.partial(_attn_kernel, head_dim=head_dim)
    return pl.pallas_call(
        kern,
        grid=(n_groups, batch, nqb),
        in_specs=[pl.BlockSpec((r, gw), q_map)] + k_specs + v_specs + [
            pl.BlockSpec((ATT_HEAD_GROUP, r, nblk * r),
                         lambda g, b, qb: (layer * n_groups + g, 0, 0),
                         pipeline_mode=pl.Buffered(1))],
        out_specs=pl.BlockSpec((r, gw), q_map),
        out_shape=jax.ShapeDtypeStruct((m, d), BF16),
        compiler_params=_compiler_params(("arbitrary", "arbitrary", "arbitrary"),
                                         48 * 1024 * 1024),
        name="band_attention",
    )(qkv, qkv, qkv, qkv, qkv, qkv, qkv, bias)


def _pool_kernel(x_ref, g_ref, w_ref, sc_ref, o_ref, ext_ref, lvl_ref, *, tiles_per_seq, windows):
    i = pl.program_id(0)
    tm, d = x_ref.shape
    halo = POOL_HALO
    n = halo + tm
    x = x_ref[...]
    h = _rms_rows(x, g_ref[...])
    first = (i % tiles_per_seq) == 0

    @pl.when(first)
    def _():
        ext_ref[0:halo, :] = jnp.zeros((halo, d), F32)

    @pl.when(jnp.logical_not(first))
    def _():
        ext_ref[0:halo, :] = ext_ref[tm:tm + halo, :]

    ext_ref[halo:halo + tm, :] = h
    pos = (i % tiles_per_seq) * tm + lax.broadcasted_iota(jnp.int32, (tm, 1), 0)
    dg = d // len(windows)
    for g, w in enumerate(windows):
        cs = slice(g * dg, (g + 1) * dg)
        levels = w.bit_length() - 1
        assert w == 1 << levels and SUBLANES * levels <= halo
        src = ext_ref.at[:, cs]
        for k in range(levels):
            lo = SUBLANES * (k + 1)
            shift = 1 << k
            summed = src[lo:n, :] + src[pl.ds(lo - shift, n - lo), :]
            if k + 1 < levels:
                dst = lvl_ref.at[k % 2]
                dst[lo:n, :] = summed
                src = dst
        acc = summed[halo - SUBLANES * levels:, :]
        hg = h[:, cs]
        cnt = jnp.minimum(pos + 1, w).astype(F32)
        pooled = acc / cnt - hg
        y = _dot(pooled.astype(BF16), w_ref[g]) * sc_ref[:, cs]
        o_ref[:, cs] = x[:, cs] + y


def _pool_mixer(x, gain, pool_w, pool_scale, seq):
    m, d = x.shape
    g, dg, _ = pool_w.shape
    halo = POOL_HALO
    kern = functools.partial(_pool_kernel, tiles_per_seq=seq // POOL_TM, windows=POOL_WINDOWS)
    return pl.pallas_call(
        kern,
        grid=(m // POOL_TM,),
        in_specs=[
            pl.BlockSpec((POOL_TM, d), lambda i: (i, 0)),
            pl.BlockSpec((1, d), lambda i: (0, 0)),
            pl.BlockSpec((g, dg, dg), lambda i: (0, 0, 0)),
            pl.BlockSpec((1, d), lambda i: (0, 0)),
        ],
        out_specs=pl.BlockSpec((POOL_TM, d), lambda i: (i, 0)),
        out_shape=jax.ShapeDtypeStruct((m, d), F32),
        scratch_shapes=[pltpu.VMEM((POOL_TM + halo, d), F32),
                        pltpu.VMEM((2, POOL_TM + halo, dg), F32)],
        compiler_params=_compiler_params(("arbitrary",), 40 * 1024 * 1024),
        name="pool_mixer",
    )(x, gain, pool_w, pool_scale)


def _mm_inv(p, q):
    return _dot(p.astype(BF16), q.astype(BF16))


def _gdn_kernel(q_ref, k_ref, v_ref, gb_ref, rows_ref, gate_ref, srows_ref, og_ref, o_ref,
                s_ref, gq_ref, ho_ref, *, n_steps, n_kheads, n_blocks, n_chunks, n_heads, dv):
    t = pl.program_id(0)
    slot_p = t % 2
    slot_s = 1 - slot_p
    hk = (jnp.minimum(t, n_steps - 1) // n_blocks) % n_kheads
    scan_block = jnp.maximum(t - 1, 0) % n_blocks

    c2 = 2 * CHUNK
    blk = c2 + CHUNK
    ri = lax.broadcasted_iota(jnp.int32, (c2, c2), 0)
    ci = lax.broadcasted_iota(jnp.int32, (c2, c2), 1)
    same = (ri // CHUNK) == (ci // CHUNK)
    causal = same & (ri >= ci)
    strict = same & (ri > ci)
    eye = (ri == ci).astype(F32)
    head0_lanes = ci < CHUNK
    lane = lax.broadcasted_iota(jnp.int32, (CHUNK, LANES), 1)
    og = og_ref[...]
    n = range(n_chunks)

    def col(gb, idx):
        return jnp.sum(jnp.where(lane == idx, gb, 0.0), axis=1, keepdims=True)

    def prepare():
        r0 = [c * CHUNK for c in n]
        q = [q_ref[r:r + CHUNK, :] for r in r0]
        k = [k_ref[r:r + CHUNK, :] for r in r0]
        gb = [gb_ref[r:r + CHUNK, :] for r in r0]
        rows = [rows_ref[0, c] for c in n]
        grow = [x[0:1] for x in rows]
        brow = [x[1:2] for x in rows]
        glast = [x[2:3] for x in rows]
        gcol = [jnp.concatenate([col(g, 2 * hk), col(g, 2 * hk + 1)], axis=0) for g in gb]
        bcol = [jnp.concatenate([col(g, n_heads + 2 * hk), col(g, n_heads + 2 * hk + 1)], axis=0)
                for g in gb]
        yield
        k2 = [jnp.concatenate([x, x], axis=0) for x in k]
        qkk = [_dot_nt(jnp.concatenate([q[i], k[i]], axis=0), k2[i]) for i in n]
        yield
        qk = [jnp.concatenate([x[:CHUNK], x[:CHUNK]], axis=0) for x in qkk]
        kk = [jnp.concatenate([x[CHUNK:], x[CHUNK:]], axis=0) for x in qkk]
        decay = [jnp.where(causal, jnp.exp(jnp.where(causal, gcol[i] - grow[i], 0.0)), 0.0)
                 for i in n]
        a = [jnp.where(strict, bcol[i] * kk[i] * decay[i], 0.0) for i in n]
        attn = [qk[i] * decay[i] for i in n]
        yield
        base = SUBLANES
        bmask = (ri // base) == (ci // base)
        a0 = [jnp.where(bmask, m, 0.0) for m in a]
        p = [_mm_inv(m, m) for m in a0]
        x = [eye - m for m in a0]
        yield
        x = [xi + _mm_inv(xi, pi) for xi, pi in zip(x, p)]
        yield
        p = [_mm_inv(pi, pi) for pi in p]
        yield
        x = [xi + _mm_inv(xi, pi) for xi, pi in zip(x, p)]
        yield
        s = base
        while s < CHUNK:
            nmask = ((ri // (2 * s)) == (ci // (2 * s))) & ((ri // s) != (ci // s))
            y = [_mm_inv(jnp.where(nmask, ai, 0.0), xi) for ai, xi in zip(a, x)]
            yield
            x = [xi - _mm_inv(xi, yi) for xi, yi in zip(x, y)]
            yield
            s *= 2
        v = [v_ref[r:r + CHUNK, :] for r in r0]
        v2 = [jnp.concatenate([m[:, :dv], m[:, dv:]], axis=0) for m in v]
        u = [_dot((x[i] * brow[i]).astype(BF16), v2[i]) for i in n]
        yield
        w = [_dot((x[i] * (brow[i] * jnp.exp(grow[i]))).astype(BF16), k2[i]) for i in n]
        yield
        wu = [jnp.concatenate([w[i], u[i]], axis=1).astype(BF16) for i in n]
        kte = [k2[i].astype(F32).T * jnp.exp(glast[i] - grow[i]) for i in n]
        lhs = [jnp.concatenate([
            jnp.where(head0_lanes, kte[i], 0.0),
            jnp.where(head0_lanes, 0.0, kte[i]),
            attn[i]], axis=0).astype(BF16) for i in n]
        yield
        r = [_dot(lhs[i], wu[i]) for i in n]
        yield
        for i in n:
            q2f = jnp.concatenate([q[i], q[i]], axis=0).astype(F32)
            qmat = q2f * jnp.exp(gcol[i]) - r[i][2 * c2:, :dv]
            omat = r[i][2 * c2:, dv:]
            gq_ref[slot_p, i] = jnp.concatenate([
                -r[i][:c2, :dv], qmat[:CHUNK], -r[i][c2:2 * c2, :dv], qmat[CHUNK:]],
                axis=0).astype(BF16)
            ho_ref[slot_p, i] = jnp.concatenate([
                r[i][:c2, dv:], omat[:CHUNK], r[i][c2:2 * c2, dv:], omat[CHUNK:]], axis=0)
        yield

    def scan():
        fresh = scan_block == 0
        states = [jnp.where(fresh, 0.0, s_ref[j]) for j in range(2)]
        for c in n:
            rows = srows_ref[0, c]
            gate = gate_ref[c * CHUNK:(c + 1) * CHUNK, :]
            outs = []
            for j in range(2):
                s = states[j]
                r = _dot(gq_ref[slot_s, c, j * blk:(j + 1) * blk, :], s.astype(BF16))
                ho = ho_ref[slot_s, c, j * blk:(j + 1) * blk, :]
                dec = jnp.exp(rows[3 + j:4 + j])
                states[j] = s * dec + r[:c2] + ho[:c2]
                o = r[c2:] + ho[c2:]
                on = o * lax.rsqrt(jnp.mean(o * o, axis=-1, keepdims=True) + EPS) * og
                outs.append(on * _silu(gate[:, j * dv:(j + 1) * dv].astype(F32)))
            o_ref[c * CHUNK:(c + 1) * CHUNK, :] = jnp.concatenate(outs, axis=1).astype(o_ref.dtype)
            yield
        s_ref[0] = states[0]
        s_ref[1] = states[1]
        yield

    def run(generators, weights):
        live = list(generators)
        while any(g is not None for g in live):
            for i, g in enumerate(live):
                for _ in range(weights[i]):
                    if live[i] is not None and next(live[i], "done") == "done":
                        live[i] = None

    @pl.when(t == 0)
    def _():
        s_ref[...] = jnp.zeros(s_ref.shape, F32)
        run([prepare()], [1])

    @pl.when((t > 0) & (t < n_steps))
    def _():
        run([prepare(), scan()], [2, 1])

    @pl.when(t == n_steps)
    def _():
        run([scan()], [1])


def _gdn_rows(gb, n_heads, n_kheads):
    m = gb.shape[0]
    nc = m // CHUNK
    rep = n_heads // n_kheads
    assert rep * CHUNK == LANES

    def stack(x):
        x = x.reshape(nc, CHUNK, n_kheads, rep).transpose(2, 0, 3, 1)
        return x.reshape(n_kheads, nc, rep * CHUNK)

    gc = gb[:, :n_heads].reshape(nc, CHUNK, n_heads)
    beta = gb[:, n_heads:2 * n_heads].reshape(nc, CHUNK, n_heads)
    last = jnp.broadcast_to(gc[:, CHUNK - 1:, :], gc.shape)
    last_k = gc[:, CHUNK - 1, :].reshape(nc, n_kheads, rep).transpose(1, 0, 2)
    per_head = [jnp.broadcast_to(last_k[:, :, r:r + 1], (n_kheads, nc, LANES)) for r in range(rep)]
    rows = [stack(gc), stack(beta), stack(last)] + per_head
    rows += [jnp.zeros_like(rows[0])] * (SUBLANES - len(rows))
    return jnp.stack(rows, axis=2)


def _gated_delta(proj, gb, rows, o_gain, batch, seq, n_kheads, n_heads, dk, dv):
    m = proj.shape[0]
    cb = GDN_CHUNKS_PER_STEP
    rb = cb * CHUNK
    ncb = seq // rb
    key_dim = n_kheads * dk
    val_dim = n_heads * dv
    vw = 2 * dv
    n_steps = batch * n_kheads * ncb
    assert n_heads == 2 * n_kheads and dk == LANES and dv == LANES

    def decode(idx):
        return (idx // (n_kheads * ncb)) * ncb + idx % ncb, (idx // ncb) % n_kheads

    def prep(col0):
        def f(t):
            row, hk = decode(jnp.minimum(t, n_steps - 1))
            return row, col0 + hk
        return f

    def lagged(col0):
        def f(t):
            row, hk = decode(jnp.maximum(t - 1, 0))
            return row, col0 + hk
        return f

    def rows_map(f):
        def g(t):
            row, hk = f(t)
            return hk, row, 0, 0
        return g

    kern = functools.partial(_gdn_kernel, n_steps=n_steps, n_kheads=n_kheads, n_blocks=ncb,
                             n_chunks=cb, n_heads=n_heads, dv=dv)
    return pl.pallas_call(
        kern,
        grid=(n_steps + 1,),
        in_specs=[
            pl.BlockSpec((rb, dk), prep(0)),
            pl.BlockSpec((rb, dk), prep(key_dim // dk)),
            pl.BlockSpec((rb, vw), prep(2 * key_dim // vw)),
            pl.BlockSpec((rb, LANES), lambda t: (prep(0)(t)[0], 0)),
            pl.BlockSpec((1, cb, SUBLANES, LANES), rows_map(prep(0))),
            pl.BlockSpec((rb, vw), lagged((2 * key_dim + val_dim) // vw)),
            pl.BlockSpec((1, cb, SUBLANES, LANES), rows_map(lagged(0))),
            pl.BlockSpec((1, dv), lambda t: (0, 0)),
        ],
        out_specs=pl.BlockSpec((rb, vw), lagged(0)),
        out_shape=jax.ShapeDtypeStruct((m, val_dim), BF16),
        scratch_shapes=[
            pltpu.VMEM((2, dk, dv), F32),
            pltpu.VMEM((2, cb, 2 * (dk + CHUNK), dv), BF16),
            pltpu.VMEM((2, cb, 2 * (dk + CHUNK), dv), F32),
        ],
        compiler_params=_compiler_params(("arbitrary",), 32 * 1024 * 1024),
        name="gated_delta",
    )(proj, proj, proj, gb, rows, proj, rows, o_gain)


def _pad_cols(w, n):
    return jnp.pad(w, ((0, 0), (0, n - w.shape[1])))


def _attention_layer(x, gain, w_qkv, q_gain, k_gain, bias, layer, w_o, batch, seq):
    d = x.shape[1]
    head_dim = d // ATT_HEADS
    qk_gain = jnp.zeros((SUBLANES, head_dim), F32)
    qk_gain = qk_gain.at[0].set(q_gain.astype(F32) * (head_dim ** -0.5 * LOG2E)).at[1].set(k_gain.astype(F32))
    qkv = _qkv_proj(x, gain, w_qkv, qk_gain, head_dim)
    o = _band_attention(qkv, bias, layer, batch, seq, d, head_dim)
    return _mm_resid(o, w_o, x)


def _gdn_layer(x, gain, w_in, conv_w, a_log, dt_bias, o_gain, w_o, batch, seq):
    d = x.shape[1]
    dk = d // GDN_K_HEADS
    dv = dk
    key_dim = GDN_K_HEADS * dk
    val_dim = GDN_V_HEADS * dv
    conv_ch = 2 * key_dim + val_dim
    n_main = conv_ch + val_dim
    proj = _gdn_proj(x, gain, w_in[:, :n_main], conv_w.astype(F32), seq, key_dim, dk)
    w_ab = _pad_cols(w_in[:, n_main:], LANES)
    pad = LANES - GDN_V_HEADS
    alog = jnp.pad(a_log.astype(F32), (0, pad)).reshape(1, LANES)
    dtb = jnp.pad(dt_bias.astype(F32), (0, pad)).reshape(1, LANES)
    gb = _gdn_gates(x, gain, w_ab, alog, dtb, GDN_V_HEADS)
    rows = _gdn_rows(gb, GDN_V_HEADS, GDN_K_HEADS)
    o = _gated_delta(proj, gb, rows, o_gain.astype(F32).reshape(1, dv), batch, seq,
                     GDN_K_HEADS, GDN_V_HEADS, dk, dv)
    return _mm_resid(o, w_o, x)


def _conv_ffn(x, gain, w_up, conv_w, w_down, seq):
    d_ff = conv_w.shape[1]
    d_ff_pad = -(-d_ff // SUB) * SUB
    wu = _pad_cols(w_up[:, :d_ff], d_ff_pad)
    wg = _pad_cols(w_up[:, d_ff:], d_ff_pad)
    cw = _pad_cols(conv_w.astype(F32), d_ff_pad)
    wd = jnp.pad(w_down, ((0, d_ff_pad - d_ff), (0, 0)))
    return _conv_ffn_call(x, gain, wu, wg, cw, wd, seq)


def kernel(x, mix_norm, ffn_norm, att_w_qkv, att_q_gain, att_k_gain, att_rel_bias, att_w_o,
           pool_w, pool_scale, gdn_w_in, gdn_conv, gdn_a_log, gdn_dt_bias, gdn_o_gain,
           gdn_w_o, ffn_w_up, ffn_conv, ffn_w_down):
    batch, seq, d = x.shape
    depth = mix_norm.shape[0]
    assert seq % TM == 0 and seq % POOL_TM == 0 and seq % ATT_ROWS == 0
    assert seq % (GDN_CHUNKS_PER_STEP * CHUNK) == 0
    xf = x.reshape(batch * seq, d).astype(F32)
    att_w_qkv, att_w_o, pool_w, gdn_w_in, gdn_w_o, ffn_w_up, ffn_w_down = (
        w.astype(BF16) for w in (att_w_qkv, att_w_o, pool_w, gdn_w_in, gdn_w_o, ffn_w_up, ffn_w_down))
    att_bias = _bias_table(att_rel_bias.reshape(-1, att_rel_bias.shape[-1]), ATT_ROWS,
                           ATT_ROWS + LEFT_CHUNKS * CHUNK)
    for i in range(depth):
        kind = i % N_MIXERS
        j = i // N_MIXERS
        gain = mix_norm[i].astype(F32).reshape(1, d)
        if kind == 0:
            xf = _attention_layer(xf, gain, att_w_qkv[j], att_q_gain[j], att_k_gain[j],
                                  att_bias, j, att_w_o[j], batch, seq)
        elif kind == 1:
            xf = _pool_mixer(xf, gain, pool_w[j],
                             pool_scale[j].astype(F32).reshape(1, d), seq)
        else:
            xf = _gdn_layer(xf, gain, gdn_w_in[j], gdn_conv[j], gdn_a_log[j], gdn_dt_bias[j],
                            gdn_o_gain[j], gdn_w_o[j], batch, seq)
        fgain = ffn_norm[i].astype(F32).reshape(1, d)
        xf = _conv_ffn(xf, fgain, ffn_w_up[i], ffn_conv[i], ffn_w_down[i], seq)
    return xf.reshape(batch, seq, d).astype(x.dtype)
```

```python
import functools

import numpy as np
import jax
import jax.numpy as jnp
from jax import lax
from jax.experimental import pallas as pl
from jax.experimental.pallas import tpu as pltpu

F32 = jnp.float32
BF16 = jnp.bfloat16

CHUNK = 64
EPS = 1e-6
MASK_VALUE = -1e30
LOG2E = 1.4426950408889634
N_MIXERS = 3

ATT_HEADS = 16
LEFT_CHUNKS = 8
MAX_REL = 256

POOL_WINDOWS = (2, 4, 8, 16)

GDN_K_HEADS = 16
GDN_V_HEADS = 32

LANES = 128
SUBLANES = 8
V7X_VMEM_BYTES = 64 * 1024 * 1024

TM = 1024
TN = 2048
SUB = 512
POOL_TM = 512
POOL_HALO = 32
ATT_ROWS = 256
ATT_HEAD_GROUP = 16
GDN_CHUNKS_PER_STEP = 16


def _compiler_params(semantics, vmem_bytes):
    assert vmem_bytes < V7X_VMEM_BYTES
    return pltpu.CompilerParams(dimension_semantics=semantics, vmem_limit_bytes=vmem_bytes)


def _rms_rows(x, gain):
    ms = jnp.mean(x * x, axis=-1, keepdims=True)
    return x * lax.rsqrt(ms + EPS) * gain


def _silu(x):
    return x * jax.nn.sigmoid(x)


def _dot(a, b):
    return jnp.dot(a, b, preferred_element_type=F32)


def _dot_nt(a, b):
    return lax.dot_general(a, b, (((1,), (1,)), ((), ())), preferred_element_type=F32)


def _causal_conv_tile(u, cw, tail_ref, ext_ref, t, first):
    tm = u.shape[0]
    k = cw.shape[0]
    prev = jnp.where(first, 0.0, tail_ref[t])
    ext_ref[0:SUBLANES, :] = prev
    ext_ref[SUBLANES:SUBLANES + tm, :] = u
    tail_ref[t] = u[tm - SUBLANES:, :]
    acc = u * cw[k - 1:k, :]
    for s in range(1, k):
        acc = acc + ext_ref[pl.ds(SUBLANES - s, tm), :] * cw[k - 1 - s:k - s, :]
    return acc


def _head_norm_store(y, o_ref, col0, head_dim, gain, mean):
    for hh in range(y.shape[1] // head_dim):
        sl = slice(hh * head_dim, (hh + 1) * head_dim)
        yh = y[:, sl]
        ss = jnp.sum(yh * yh, axis=-1, keepdims=True)
        r = lax.rsqrt((ss / head_dim if mean else ss) + EPS)
        o_ref[:, col0 + hh * head_dim:col0 + (hh + 1) * head_dim] = (yh * r * gain).astype(o_ref.dtype)


def _qkv_kernel(x_ref, g_ref, w_ref, qkg_ref, o_ref, h_ref, *, steps_per_kind, head_dim):
    j = pl.program_id(1)

    @pl.when(j == 0)
    def _():
        h_ref[...] = _rms_rows(x_ref[...], g_ref[...]).astype(BF16)

    kind = j // steps_per_kind
    pieces = range(o_ref.shape[1] // SUB)

    @pl.when(kind < 2)
    def _():
        gain = qkg_ref[pl.ds(kind, 1), :]
        for t in pieces:
            y = _dot(h_ref[...], w_ref[:, t * SUB:(t + 1) * SUB])
            _head_norm_store(y, o_ref, t * SUB, head_dim, gain, mean=True)

    @pl.when(kind == 2)
    def _():
        for t in pieces:
            cols = slice(t * SUB, (t + 1) * SUB)
            o_ref[:, cols] = _dot(h_ref[...], w_ref[:, cols]).astype(o_ref.dtype)


def _qkv_proj(x, gain, w, qk_gain, head_dim):
    m, d = x.shape
    n = w.shape[1]
    kern = functools.partial(_qkv_kernel, steps_per_kind=d // TN, head_dim=head_dim)
    return pl.pallas_call(
        kern,
        grid=(m // TM, n // TN),
        in_specs=[
            pl.BlockSpec((TM, d), lambda i, j: (i, 0)),
            pl.BlockSpec((1, d), lambda i, j: (0, 0)),
            pl.BlockSpec((d, TN), lambda i, j: (0, j)),
            pl.BlockSpec((SUBLANES, head_dim), lambda i, j: (0, 0)),
        ],
        out_specs=pl.BlockSpec((TM, TN), lambda i, j: (i, j)),
        out_shape=jax.ShapeDtypeStruct((m, n), BF16),
        scratch_shapes=[pltpu.VMEM((TM, d), BF16)],
        compiler_params=_compiler_params(("arbitrary", "arbitrary"), 56 * 1024 * 1024),
        name="qkv_proj",
    )(x, gain, w, qk_gain)


def _ffn_kernel(x_ref, g_ref, wu_ref, wg_ref, cw_ref, wd_ref, o_ref, h_ref, tail_ref, ext_ref,
                *, tiles_per_seq):
    i = pl.program_id(0)
    j = pl.program_id(1)

    @pl.when(j == 0)
    def _():
        x = x_ref[...]
        h_ref[...] = _rms_rows(x, g_ref[...]).astype(BF16)
        o_ref[...] = x

    h = h_ref[...]
    u = _dot(h, wu_ref[...])
    gate = _dot(h, wg_ref[...])
    first = (i % tiles_per_seq) == 0
    c = _causal_conv_tile(u, cw_ref[...], tail_ref, ext_ref, j, first)
    act = (_silu(c) * gate).astype(BF16)
    o_ref[...] += _dot(act, wd_ref[...])


def _conv_ffn_call(x, gain, wu, wg, cw, wd, seq):
    m, d = x.shape
    n = wg.shape[1]
    kern = functools.partial(_ffn_kernel, tiles_per_seq=seq // TM)
    return pl.pallas_call(
        kern,
        grid=(m // TM, n // SUB),
        in_specs=[
            pl.BlockSpec((TM, d), lambda i, j: (i, 0)),
            pl.BlockSpec((1, d), lambda i, j: (0, 0)),
            pl.BlockSpec((d, SUB), lambda i, j: (0, j)),
            pl.BlockSpec((d, SUB), lambda i, j: (0, j)),
            pl.BlockSpec((cw.shape[0], SUB), lambda i, j: (0, j)),
            pl.BlockSpec((SUB, d), lambda i, j: (j, 0)),
        ],
        out_specs=pl.BlockSpec((TM, d), lambda i, j: (i, 0)),
        out_shape=jax.ShapeDtypeStruct((m, d), F32),
        scratch_shapes=[
            pltpu.VMEM((TM, d), BF16),
            pltpu.VMEM((n // SUB, SUBLANES, SUB), F32),
            pltpu.VMEM((TM + SUBLANES, SUB), F32),
        ],
        compiler_params=_compiler_params(("arbitrary", "arbitrary"), 56 * 1024 * 1024),
        name="conv_ffn",
    )(x, gain, wu, wg, cw, wd)


def _gdn_proj_kernel(x_ref, g_ref, w_ref, cw_ref, o_ref, h_ref, tail_ref, ext_ref,
                     *, tiles_per_seq, n_head_steps, n_conv_steps, head_dim):
    i = pl.program_id(0)
    j = pl.program_id(1)

    @pl.when(j == 0)
    def _():
        h_ref[...] = _rms_rows(x_ref[...], g_ref[...]).astype(BF16)

    n_pieces = o_ref.shape[1] // SUB
    first = (i % tiles_per_seq) == 0

    def conv_piece(t):
        cols = slice(t * SUB, (t + 1) * SUB)
        y = _dot(h_ref[...], w_ref[:, cols])
        return _silu(_causal_conv_tile(y, cw_ref[:, cols], tail_ref, ext_ref.at[t],
                                       j * n_pieces + t, first))

    @pl.when(j < 2 * n_head_steps)
    def _():
        scale = jnp.where(j < n_head_steps, head_dim ** -0.5, 1.0)
        for t in range(n_pieces):
            _head_norm_store(conv_piece(t), o_ref, t * SUB, head_dim, scale, mean=False)

    @pl.when((j >= 2 * n_head_steps) & (j < n_conv_steps))
    def _():
        for t in range(n_pieces):
            o_ref[:, t * SUB:(t + 1) * SUB] = conv_piece(t).astype(o_ref.dtype)

    @pl.when(j >= n_conv_steps)
    def _():
        for t in range(n_pieces):
            cols = slice(t * SUB, (t + 1) * SUB)
            o_ref[:, cols] = _dot(h_ref[...], w_ref[:, cols]).astype(o_ref.dtype)


def _gdn_proj(x, gain, w, n, cw, seq, key_dim, head_dim):
    m, d = x.shape
    assert n % TN == 0 and n <= w.shape[1]
    n_conv_steps = cw.shape[1] // TN
    n_pieces = TN // SUB
    kern = functools.partial(
        _gdn_proj_kernel, tiles_per_seq=seq // TM, n_head_steps=key_dim // TN,
        n_conv_steps=n_conv_steps, head_dim=head_dim)
    return pl.pallas_call(
        kern,
        grid=(m // TM, n // TN),
        in_specs=[
            pl.BlockSpec((TM, d), lambda i, j: (i, 0)),
            pl.BlockSpec((1, d), lambda i, j: (0, 0)),
            pl.BlockSpec((d, TN), lambda i, j: (0, j)),
            pl.BlockSpec((cw.shape[0], TN), lambda i, j: (0, jnp.minimum(j, n_conv_steps - 1))),
        ],
        out_specs=pl.BlockSpec((TM, TN), lambda i, j: (i, j)),
        out_shape=jax.ShapeDtypeStruct((m, n), BF16),
        scratch_shapes=[
            pltpu.VMEM((TM, d), BF16),
            pltpu.VMEM((n_conv_steps * n_pieces, SUBLANES, SUB), F32),
            pltpu.VMEM((n_pieces, TM + SUBLANES, SUB), F32),
        ],
        compiler_params=_compiler_params(("arbitrary", "arbitrary"), 58 * 1024 * 1024),
        name="gdn_proj",
    )(x, gain, w, cw)


def _gdn_gate_kernel(x_ref, g_ref, w_ref, alog_ref, dtb_ref, o_ref, *, n_heads):
    h = _rms_rows(x_ref[...], g_ref[...]).astype(BF16)
    y = _dot(h, w_ref[...])
    is_decay = lax.broadcasted_iota(jnp.int32, (1, y.shape[1]), 1) < n_heads
    g = -jnp.exp(alog_ref[...]) * jax.nn.softplus(y + dtb_ref[...])
    vals = jnp.where(is_decay, g, jax.nn.sigmoid(y))
    ri = lax.broadcasted_iota(jnp.int32, (CHUNK, CHUNK), 0)
    ci = lax.broadcasted_iota(jnp.int32, (CHUNK, CHUNK), 1)
    tri = (ri >= ci).astype(F32)
    for c in range(y.shape[0] // CHUNK):
        rows = slice(c * CHUNK, (c + 1) * CHUNK)
        v = vals[rows, :]
        cum = jnp.dot(tri, v, precision=lax.Precision.HIGHEST, preferred_element_type=F32)
        o_ref[rows, :] = jnp.where(is_decay, cum, v)


def _gdn_gates(x, gain, w_ab, alog, dtb, n_heads):
    m, d = x.shape
    kern = functools.partial(_gdn_gate_kernel, n_heads=n_heads)
    return pl.pallas_call(
        kern,
        grid=(m // TM,),
        in_specs=[
            pl.BlockSpec((TM, d), lambda i: (i, 0)),
            pl.BlockSpec((1, d), lambda i: (0, 0)),
            pl.BlockSpec((d, LANES), lambda i: (0, 0)),
            pl.BlockSpec((1, LANES), lambda i: (0, 0)),
            pl.BlockSpec((1, LANES), lambda i: (0, 0)),
        ],
        out_specs=pl.BlockSpec((TM, LANES), lambda i: (i, 0)),
        out_shape=jax.ShapeDtypeStruct((m, LANES), F32),
        compiler_params=_compiler_params(("arbitrary",), 32 * 1024 * 1024),
        name="gdn_gates",
    )(x, gain, w_ab, alog, dtb)


def _mm_resid_kernel(a_ref, w_ref, r_ref, o_ref):
    o_ref[...] = r_ref[...] + _dot(a_ref[...], w_ref[...])


def _mm_resid(a, w, resid):
    m, k = a.shape
    n = w.shape[1]
    tm = TM // 2
    return pl.pallas_call(
        _mm_resid_kernel,
        grid=(m // tm,),
        in_specs=[
            pl.BlockSpec((tm, k), lambda i: (i, 0)),
            pl.BlockSpec((k, n), lambda i: (0, 0), pipeline_mode=pl.Buffered(1)),
            pl.BlockSpec((tm, n), lambda i: (i, 0)),
        ],
        out_specs=pl.BlockSpec((tm, n), lambda i: (i, 0)),
        out_shape=jax.ShapeDtypeStruct((m, n), F32),
        compiler_params=_compiler_params(("arbitrary",), 48 * 1024 * 1024),
        name="mm_resid",
    )(a, w, resid)


def _bias_table_kernel(rb_ref, idx_ref, add_ref, o_ref):
    nrel = rb_ref.shape[1]
    rb = rb_ref[...]
    hi = rb.astype(BF16)
    rest = rb - hi.astype(F32)
    mid = rest.astype(BF16)
    lo = (rest - mid.astype(F32)).astype(BF16)
    rel = lax.broadcasted_iota(jnp.int32, (nrel, idx_ref.shape[1]), 0)
    for r in range(idx_ref.shape[0]):
        onehot = jnp.where(rel == idx_ref[r:r + 1, :], 1.0, 0.0).astype(BF16)
        picked = _dot(hi, onehot) + _dot(mid, onehot) + _dot(lo, onehot)
        o_ref[:, r, :] = picked * LOG2E + add_ref[r:r + 1, :]


def _band_layout(rows, keys):
    qpos = np.arange(rows)[:, None]
    kpos = np.arange(keys)[None, :] - (keys - rows)
    qc = qpos // CHUNK
    kc = np.floor_divide(kpos, CHUNK)
    in_band = (kc <= qc) & (kc >= qc - LEFT_CHUNKS)
    idx = np.clip(qpos - kpos, -(CHUNK - 1), MAX_REL) + (CHUNK - 1)
    idx = np.where(in_band, idx, 0).astype(np.int32)
    add = np.where(in_band, 0.0, MASK_VALUE).astype(np.float32)
    return idx, add


def _bias_table(rel_bias, rows, keys):
    heads, nrel = rel_bias.shape
    nrel_pad = -(-nrel // LANES) * LANES
    rb = jnp.pad(rel_bias.astype(F32), ((0, 0), (0, nrel_pad - nrel)))
    idx, add = _band_layout(rows, keys)
    tr = SUBLANES
    return pl.pallas_call(
        _bias_table_kernel,
        grid=(rows // tr,),
        in_specs=[
            pl.BlockSpec((heads, nrel_pad), lambda i: (0, 0)),
            pl.BlockSpec((tr, keys), lambda i: (i, 0)),
            pl.BlockSpec((tr, keys), lambda i: (i, 0)),
        ],
        out_specs=pl.BlockSpec((heads, tr, keys), lambda i: (0, i, 0)),
        out_shape=jax.ShapeDtypeStruct((heads, rows, keys), F32),
        compiler_params=_compiler_params(("arbitrary",), 32 * 1024 * 1024),
        name="att_bias_table",
    )(rb, jnp.asarray(idx), jnp.asarray(add))


def _attn_kernel(q_ref, k0_ref, k1_ref, k2_ref, v0_ref, v1_ref, v2_ref, bias_ref, o_ref,
                 *, head_dim):
    qb = pl.program_id(2)
    rows = q_ref.shape[0]
    k_refs = (k0_ref, k1_ref, k2_ref)
    v_refs = (v0_ref, v1_ref, v2_ref)
    nblk = len(k_refs)
    heads = range(q_ref.shape[1] // head_dim)
    sl = [slice(hh * head_dim, (hh + 1) * head_dim) for hh in heads]
    def qk(hh):
        q = q_ref[:, sl[hh]]
        row = []
        for t in range(nblk):
            s = _dot_nt(q, k_refs[t][:, sl[hh]]) + bias_ref[hh, :, t * rows:(t + 1) * rows]
            if t < nblk - 1:
                s = jnp.where(qb >= nblk - 1 - t, s, MASK_VALUE)
            row.append(s)
        return row

    def fold(parts, op):
        m = functools.reduce(op, parts)
        return functools.reduce(op, [m[:, i:i + LANES] for i in range(0, m.shape[1], LANES)])

    ones = jnp.ones((rows, head_dim), BF16)

    def softmax(sc):
        mx = fold(sc, jnp.maximum).max(axis=-1, keepdims=True)
        return [jnp.exp2((s - mx).astype(BF16)) for s in sc]

    def pv(hh, p):
        acc = sum(_dot(p[t], jnp.concatenate([v_refs[t][:, sl[hh]], ones], axis=1))
                  for t in range(nblk))
        o_ref[:, sl[hh]] = (acc[:, :head_dim] / acc[:, head_dim:]).astype(o_ref.dtype)

    sc = qk(0)
    for hh in heads:
        nxt = qk(hh + 1) if hh + 1 < len(heads) else None
        pv(hh, softmax(sc))
        sc = nxt


def _band_attention(qkv, bias, layer, batch, seq, d, head_dim):
    m = qkv.shape[0]
    r = ATT_ROWS
    gw = ATT_HEAD_GROUP * head_dim
    n_groups = d // gw
    nqb = seq // r
    nblk = bias.shape[2] // r

    def q_map(g, b, qb):
        return (b * nqb + qb, g)

    def kv_map(off, col0):
        def f(g, b, qb):
            return (b * nqb + jnp.maximum(qb - off, 0), col0 + g)
        return f

    k_specs = [pl.BlockSpec((r, gw), kv_map(nblk - 1 - t, n_groups)) for t in range(nblk)]
    v_specs = [pl.BlockSpec((r, gw), kv_map(nblk - 1 - t, 2 * n_groups)) for t in range(nblk)]
    kern = functools.partial(_attn_kernel, head_dim=head_dim)
    return pl.pallas_call(
        kern,
        grid=(n_groups, batch, nqb),
        in_specs=[pl.BlockSpec((r, gw), q_map)] + k_specs + v_specs + [
            pl.BlockSpec((ATT_HEAD_GROUP, r, nblk * r),
                         lambda g, b, qb: (layer * n_groups + g, 0, 0),
                         pipeline_mode=pl.Buffered(1))],
        out_specs=pl.BlockSpec((r, gw), q_map),
        out_shape=jax.ShapeDtypeStruct((m, d), BF16),
        compiler_params=_compiler_params(("arbitrary", "arbitrary", "arbitrary"),
                                         48 * 1024 * 1024),
        name="band_attention",
    )(qkv, qkv, qkv, qkv, qkv, qkv, qkv, bias)


def _pool_kernel(x_ref, g_ref, w_ref, sc_ref, o_ref, ext_ref, lvl_ref, *, tiles_per_seq, windows):
    i = pl.program_id(0)
    tm, d = x_ref.shape
    halo = POOL_HALO
    n = halo + tm
    x = x_ref[...]
    h = _rms_rows(x, g_ref[...])
    first = (i % tiles_per_seq) == 0

    @pl.when(first)
    def _():
        ext_ref[0:halo, :] = jnp.zeros((halo, d), F32)

    @pl.when(jnp.logical_not(first))
    def _():
        ext_ref[0:halo, :] = ext_ref[tm:tm + halo, :]

    ext_ref[halo:halo + tm, :] = h
    pos = (i % tiles_per_seq) * tm + lax.broadcasted_iota(jnp.int32, (tm, 1), 0)
    dg = d // len(windows)
    for g, w in enumerate(windows):
        cs = slice(g * dg, (g + 1) * dg)
        levels = w.bit_length() - 1
        assert w == 1 << levels and SUBLANES * levels <= halo
        src = ext_ref.at[:, cs]
        for k in range(levels):
            lo = SUBLANES * (k + 1)
            shift = 1 << k
            summed = src[lo:n, :] + src[pl.ds(lo - shift, n - lo), :]
            if k + 1 < levels:
                dst = lvl_ref.at[k % 2]
                dst[lo:n, :] = summed
                src = dst
        acc = summed[halo - SUBLANES * levels:, :]
        hg = h[:, cs]
        cnt = jnp.minimum(pos + 1, w).astype(F32)
        pooled = acc / cnt - hg
        y = _dot(pooled.astype(BF16), w_ref[g]) * sc_ref[:, cs]
        o_ref[:, cs] = x[:, cs] + y


def _pool_mixer(x, gain, pool_w, pool_scale, seq):
    m, d = x.shape
    g, dg, _ = pool_w.shape
    halo = POOL_HALO
    kern = functools.partial(_pool_kernel, tiles_per_seq=seq // POOL_TM, windows=POOL_WINDOWS)
    return pl.pallas_call(
        kern,
        grid=(m // POOL_TM,),
        in_specs=[
            pl.BlockSpec((POOL_TM, d), lambda i: (i, 0)),
            pl.BlockSpec((1, d), lambda i: (0, 0)),
            pl.BlockSpec((g, dg, dg), lambda i: (0, 0, 0)),
            pl.BlockSpec((1, d), lambda i: (0, 0)),
        ],
        out_specs=pl.BlockSpec((POOL_TM, d), lambda i: (i, 0)),
        out_shape=jax.ShapeDtypeStruct((m, d), F32),
        scratch_shapes=[pltpu.VMEM((POOL_TM + halo, d), F32),
                        pltpu.VMEM((2, POOL_TM + halo, dg), F32)],
        compiler_params=_compiler_params(("arbitrary",), 40 * 1024 * 1024),
        name="pool_mixer",
    )(x, gain, pool_w, pool_scale)


def _mm_inv(p, q):
    return _dot(p.astype(BF16), q.astype(BF16))


def _gdn_kernel(q_ref, k_ref, v_ref, gb_ref, rows_ref, gate_ref, srows_ref, og_ref, o_ref,
                s_ref, gq_ref, ho_ref, *, n_steps, n_kheads, n_blocks, n_chunks, n_heads, dv):
    t = pl.program_id(0)
    slot_p = t % 2
    slot_s = 1 - slot_p
    hk = (jnp.minimum(t, n_steps - 1) // n_blocks) % n_kheads
    scan_block = jnp.maximum(t - 1, 0) % n_blocks

    c2 = 2 * CHUNK
    blk = c2 + CHUNK
    ri = lax.broadcasted_iota(jnp.int32, (c2, c2), 0)
    ci = lax.broadcasted_iota(jnp.int32, (c2, c2), 1)
    same = (ri // CHUNK) == (ci // CHUNK)
    causal = same & (ri >= ci)
    strict = same & (ri > ci)
    eye = (ri == ci).astype(F32)
    head0_lanes = ci < CHUNK
    lane = lax.broadcasted_iota(jnp.int32, (CHUNK, LANES), 1)
    og = og_ref[...]
    n = range(n_chunks)

    def col(gb, idx):
        return jnp.sum(jnp.where(lane == idx, gb, 0.0), axis=1, keepdims=True)

    def prepare():
        r0 = [c * CHUNK for c in n]
        q = [q_ref[r:r + CHUNK, :] for r in r0]
        k = [k_ref[r:r + CHUNK, :] for r in r0]
        gb = [gb_ref[r:r + CHUNK, :] for r in r0]
        rows = [rows_ref[0, c] for c in n]
        grow = [x[0:1] for x in rows]
        brow = [x[1:2] for x in rows]
        glast = [x[2:3] for x in rows]
        gcol = [jnp.concatenate([col(g, 2 * hk), col(g, 2 * hk + 1)], axis=0) for g in gb]
        bcol = [jnp.concatenate([col(g, n_heads + 2 * hk), col(g, n_heads + 2 * hk + 1)], axis=0)
                for g in gb]
        yield
        k2 = [jnp.concatenate([x, x], axis=0) for x in k]
        qkk = [_dot_nt(jnp.concatenate([q[i], k[i]], axis=0), k2[i]) for i in n]
        yield
        qk = [jnp.concatenate([x[:CHUNK], x[:CHUNK]], axis=0) for x in qkk]
        kk = [jnp.concatenate([x[CHUNK:], x[CHUNK:]], axis=0) for x in qkk]
        decay = [jnp.where(causal, jnp.exp(jnp.where(causal, gcol[i] - grow[i], 0.0)), 0.0)
                 for i in n]
        a = [jnp.where(strict, bcol[i] * kk[i] * decay[i], 0.0) for i in n]
        attn = [qk[i] * decay[i] for i in n]
        yield
        base = SUBLANES
        bmask = (ri // base) == (ci // base)
        a0 = [jnp.where(bmask, m, 0.0) for m in a]
        p = [_mm_inv(m, m) for m in a0]
        x = [eye - m for m in a0]
        yield
        x = [xi + _mm_inv(xi, pi) for xi, pi in zip(x, p)]
        yield
        p = [_mm_inv(pi, pi) for pi in p]
        yield
        x = [xi + _mm_inv(xi, pi) for xi, pi in zip(x, p)]
        yield
        s = base
        while s < CHUNK:
            nmask = ((ri // (2 * s)) == (ci // (2 * s))) & ((ri // s) != (ci // s))
            y = [_mm_inv(jnp.where(nmask, ai, 0.0), xi) for ai, xi in zip(a, x)]
            yield
            x = [xi - _mm_inv(xi, yi) for xi, yi in zip(x, y)]
            yield
            s *= 2
        v = [v_ref[r:r + CHUNK, :] for r in r0]
        v2 = [jnp.concatenate([m[:, :dv], m[:, dv:]], axis=0) for m in v]
        u = [_dot((x[i] * brow[i]).astype(BF16), v2[i]) for i in n]
        yield
        w = [_dot((x[i] * (brow[i] * jnp.exp(grow[i]))).astype(BF16), k2[i]) for i in n]
        yield
        wu = [jnp.concatenate([w[i], u[i]], axis=1).astype(BF16) for i in n]
        kte = [k2[i].astype(F32).T * jnp.exp(glast[i] - grow[i]) for i in n]
        lhs = [jnp.concatenate([
            jnp.where(head0_lanes, kte[i], 0.0),
            jnp.where(head0_lanes, 0.0, kte[i]),
            attn[i]], axis=0).astype(BF16) for i in n]
        yield
        r = [_dot(lhs[i], wu[i]) for i in n]
        yield
        for i in n:
            q2f = jnp.concatenate([q[i], q[i]], axis=0).astype(F32)
            qmat = q2f * jnp.exp(gcol[i]) - r[i][2 * c2:, :dv]
            omat = r[i][2 * c2:, dv:]
            gq_ref[slot_p, i] = jnp.concatenate([
                -r[i][:c2, :dv], qmat[:CHUNK], -r[i][c2:2 * c2, :dv], qmat[CHUNK:]],
                axis=0).astype(BF16)
            ho_ref[slot_p, i] = jnp.concatenate([
                r[i][:c2, dv:], omat[:CHUNK], r[i][c2:2 * c2, dv:], omat[CHUNK:]], axis=0)
        yield

    def scan():
        fresh = scan_block == 0
        states = [jnp.where(fresh, 0.0, s_ref[j]) for j in range(2)]
        for c in n:
            rows = srows_ref[0, c]
            gate = gate_ref[c * CHUNK:(c + 1) * CHUNK, :]
            outs = []
            for j in range(2):
                s = states[j]
                r = _dot(gq_ref[slot_s, c, j * blk:(j + 1) * blk, :], s.astype(BF16))
                ho = ho_ref[slot_s, c, j * blk:(j + 1) * blk, :]
                dec = jnp.exp(rows[3 + j:4 + j])
                states[j] = s * dec + r[:c2] + ho[:c2]
                o = r[c2:] + ho[c2:]
                on = o * lax.rsqrt(jnp.mean(o * o, axis=-1, keepdims=True) + EPS) * og
                outs.append(on * _silu(gate[:, j * dv:(j + 1) * dv].astype(F32)))
            o_ref[c * CHUNK:(c + 1) * CHUNK, :] = jnp.concatenate(outs, axis=1).astype(o_ref.dtype)
            yield
        s_ref[0] = states[0]
        s_ref[1] = states[1]
        yield

    def run(generators, weights):
        live = list(generators)
        while any(g is not None for g in live):
            for i, g in enumerate(live):
                for _ in range(weights[i]):
                    if live[i] is not None and next(live[i], "done") == "done":
                        live[i] = None

    @pl.when(t == 0)
    def _():
        s_ref[...] = jnp.zeros(s_ref.shape, F32)
        run([prepare()], [1])

    @pl.when((t > 0) & (t < n_steps))
    def _():
        run([prepare(), scan()], [2, 1])

    @pl.when(t == n_steps)
    def _():
        run([scan()], [1])


def _gdn_rows(gb, n_heads, n_kheads):
    m = gb.shape[0]
    nc = m // CHUNK
    rep = n_heads // n_kheads
    assert rep * CHUNK == LANES

    def stack(x):
        x = x.reshape(nc, CHUNK, n_kheads, rep).transpose(2, 0, 3, 1)
        return x.reshape(n_kheads, nc, rep * CHUNK)

    gc = gb[:, :n_heads].reshape(nc, CHUNK, n_heads)
    beta = gb[:, n_heads:2 * n_heads].reshape(nc, CHUNK, n_heads)
    last = jnp.broadcast_to(gc[:, CHUNK - 1:, :], gc.shape)
    last_k = gc[:, CHUNK - 1, :].reshape(nc, n_kheads, rep).transpose(1, 0, 2)
    per_head = [jnp.broadcast_to(last_k[:, :, r:r + 1], (n_kheads, nc, LANES)) for r in range(rep)]
    rows = [stack(gc), stack(beta), stack(last)] + per_head
    rows += [jnp.zeros_like(rows[0])] * (SUBLANES - len(rows))
    return jnp.stack(rows, axis=2)


def _gated_delta(proj, gb, rows, o_gain, batch, seq, n_kheads, n_heads, dk, dv):
    m = proj.shape[0]
    cb = GDN_CHUNKS_PER_STEP
    rb = cb * CHUNK
    ncb = seq // rb
    key_dim = n_kheads * dk
    val_dim = n_heads * dv
    vw = 2 * dv
    n_steps = batch * n_kheads * ncb
    assert n_heads == 2 * n_kheads and dk == LANES and dv == LANES

    def decode(idx):
        return (idx // (n_kheads * ncb)) * ncb + idx % ncb, (idx // ncb) % n_kheads

    def prep(col0):
        def f(t):
            row, hk = decode(jnp.minimum(t, n_steps - 1))
            return row, col0 + hk
        return f

    def lagged(col0):
        def f(t):
            row, hk = decode(jnp.maximum(t - 1, 0))
            return row, col0 + hk
        return f

    def rows_map(f):
        def g(t):
            row, hk = f(t)
            return hk, row, 0, 0
        return g

    kern = functools.partial(_gdn_kernel, n_steps=n_steps, n_kheads=n_kheads, n_blocks=ncb,
                             n_chunks=cb, n_heads=n_heads, dv=dv)
    return pl.pallas_call(
        kern,
        grid=(n_steps + 1,),
        in_specs=[
            pl.BlockSpec((rb, dk), prep(0)),
            pl.BlockSpec((rb, dk), prep(key_dim // dk)),
            pl.BlockSpec((rb, vw), prep(2 * key_dim // vw)),
            pl.BlockSpec((rb, LANES), lambda t: (prep(0)(t)[0], 0)),
            pl.BlockSpec((1, cb, SUBLANES, LANES), rows_map(prep(0))),
            pl.BlockSpec((rb, vw), lagged((2 * key_dim + val_dim) // vw)),
            pl.BlockSpec((1, cb, SUBLANES, LANES), rows_map(lagged(0))),
            pl.BlockSpec((1, dv), lambda t: (0, 0)),
        ],
        out_specs=pl.BlockSpec((rb, vw), lagged(0)),
        out_shape=jax.ShapeDtypeStruct((m, val_dim), BF16),
        scratch_shapes=[
            pltpu.VMEM((2, dk, dv), F32),
            pltpu.VMEM((2, cb, 2 * (dk + CHUNK), dv), BF16),
            pltpu.VMEM((2, cb, 2 * (dk + CHUNK), dv), F32),
        ],
        compiler_params=_compiler_params(("arbitrary",), 32 * 1024 * 1024),
        name="gated_delta",
    )(proj, proj, proj, gb, rows, proj, rows, o_gain)


def _pad_cols(w, n):
    return jnp.pad(w, ((0, 0), (0, n - w.shape[1])))


def _attention_layer(x, gain, w_qkv, q_gain, k_gain, bias, layer, w_o, batch, seq):
    d = x.shape[1]
    head_dim = d // ATT_HEADS
    qk_gain = jnp.zeros((SUBLANES, head_dim), F32)
    qk_gain = qk_gain.at[0].set(q_gain.astype(F32) * (head_dim ** -0.5 * LOG2E)).at[1].set(k_gain.astype(F32))
    qkv = _qkv_proj(x, gain, w_qkv.astype(BF16), qk_gain, head_dim)
    o = _band_attention(qkv, bias, layer, batch, seq, d, head_dim)
    return _mm_resid(o, w_o.astype(BF16), x)


def _gdn_layer(x, gain, w_in, conv_w, a_log, dt_bias, o_gain, w_o, batch, seq):
    d = x.shape[1]
    dk = d // GDN_K_HEADS
    dv = dk
    key_dim = GDN_K_HEADS * dk
    val_dim = GDN_V_HEADS * dv
    conv_ch = 2 * key_dim + val_dim
    n_main = conv_ch + val_dim
    w_in = w_in.astype(BF16)
    proj = _gdn_proj(x, gain, w_in, n_main, conv_w.astype(F32), seq, key_dim, dk)
    w_ab = _pad_cols(w_in[:, n_main:], LANES)
    pad = LANES - GDN_V_HEADS
    alog = jnp.pad(a_log.astype(F32), (0, pad)).reshape(1, LANES)
    dtb = jnp.pad(dt_bias.astype(F32), (0, pad)).reshape(1, LANES)
    gb = _gdn_gates(x, gain, w_ab, alog, dtb, GDN_V_HEADS)
    rows = _gdn_rows(gb, GDN_V_HEADS, GDN_K_HEADS)
    o = _gated_delta(proj, gb, rows, o_gain.astype(F32).reshape(1, dv), batch, seq,
                     GDN_K_HEADS, GDN_V_HEADS, dk, dv)
    return _mm_resid(o, w_o.astype(BF16), x)


def _conv_ffn(x, gain, w_up, conv_w, w_down, seq):
    d_ff = conv_w.shape[1]
    d_ff_pad = -(-d_ff // SUB) * SUB
    w_up = w_up.astype(BF16)
    assert d_ff_pad <= w_up.shape[1]
    wg = _pad_cols(w_up[:, d_ff:], d_ff_pad)
    cw = _pad_cols(conv_w.astype(F32), d_ff_pad)
    wd = jnp.pad(w_down.astype(BF16), ((0, d_ff_pad - d_ff), (0, 0)))
    return _conv_ffn_call(x, gain, w_up, wg, cw, wd, seq)


def kernel(x, mix_norm, ffn_norm, att_w_qkv, att_q_gain, att_k_gain, att_rel_bias, att_w_o,
           pool_w, pool_scale, gdn_w_in, gdn_conv, gdn_a_log, gdn_dt_bias, gdn_o_gain,
           gdn_w_o, ffn_w_up, ffn_conv, ffn_w_down):
    batch, seq, d = x.shape
    depth = mix_norm.shape[0]
    assert seq % TM == 0 and seq % POOL_TM == 0 and seq % ATT_ROWS == 0
    assert seq % (GDN_CHUNKS_PER_STEP * CHUNK) == 0
    xf = x.reshape(batch * seq, d).astype(F32)
    att_bias = _bias_table(att_rel_bias.reshape(-1, att_rel_bias.shape[-1]), ATT_ROWS,
                           ATT_ROWS + LEFT_CHUNKS * CHUNK)
    for i in range(depth):
        kind = i % N_MIXERS
        j = i // N_MIXERS
        gain = mix_norm[i].astype(F32).reshape(1, d)
        if kind == 0:
            xf = _attention_layer(xf, gain, att_w_qkv[j], att_q_gain[j], att_k_gain[j],
                                  att_bias, j, att_w_o[j], batch, seq)
        elif kind == 1:
            xf = _pool_mixer(xf, gain, pool_w[j].astype(BF16),
                             pool_scale[j].astype(F32).reshape(1, d), seq)
        else:
            xf = _gdn_layer(xf, gain, gdn_w_in[j], gdn_conv[j], gdn_a_log[j], gdn_dt_bias[j],
                            gdn_o_gain[j], gdn_w_o[j], batch, seq)
        fgain = ffn_norm[i].astype(F32).reshape(1, d)
        xf = _conv_ffn(xf, fgain, ffn_w_up[i], ffn_conv[i], ffn_w_down[i], seq)
    return xf.reshape(batch, seq, d).astype(x.dtype)
```

```python
import functools

import numpy as np
import jax
import jax.numpy as jnp
from jax import lax
from jax.experimental import pallas as pl
from jax.experimental.pallas import tpu as pltpu

F32 = jnp.float32
BF16 = jnp.bfloat16

CHUNK = 64
EPS = 1e-6
MASK_VALUE = -1e30
LOG2E = 1.4426950408889634
N_MIXERS = 3

ATT_HEADS = 16
LEFT_CHUNKS = 8
MAX_REL = 256

POOL_WINDOWS = (2, 4, 8, 16)

GDN_K_HEADS = 16
GDN_V_HEADS = 32

LANES = 128
SUBLANES = 8
V7X_VMEM_BYTES = 64 * 1024 * 1024

TM = 1024
TN = 2048
SUB = 512
POOL_TM = 512
POOL_HALO = 32
ATT_ROWS = 256
ATT_HEAD_GROUP = 16
GDN_CHUNKS_PER_STEP = 16


def _compiler_params(semantics, vmem_bytes):
    assert vmem_bytes < V7X_VMEM_BYTES
    return pltpu.CompilerParams(dimension_semantics=semantics, vmem_limit_bytes=vmem_bytes)


def _rms_rows(x, gain):
    ms = jnp.mean(x * x, axis=-1, keepdims=True)
    return x * lax.rsqrt(ms + EPS) * gain


def _silu(x):
    return x * jax.nn.sigmoid(x)


def _dot(a, b):
    return jnp.dot(a, b, preferred_element_type=F32)


def _dot_nt(a, b):
    return lax.dot_general(a, b, (((1,), (1,)), ((), ())), preferred_element_type=F32)


def _causal_conv_tile(u, cw, tail_ref, ext_ref, t, first):
    tm = u.shape[0]
    k = cw.shape[0]
    prev = jnp.where(first, 0.0, tail_ref[t])
    ext_ref[0:SUBLANES, :] = prev
    ext_ref[SUBLANES:SUBLANES + tm, :] = u
    tail_ref[t] = u[tm - SUBLANES:, :]
    acc = u * cw[k - 1:k, :]
    for s in range(1, k):
        acc = acc + ext_ref[pl.ds(SUBLANES - s, tm), :] * cw[k - 1 - s:k - s, :]
    return acc


def _head_norm_store(y, o_ref, col0, head_dim, gain, mean):
    for hh in range(y.shape[1] // head_dim):
        sl = slice(hh * head_dim, (hh + 1) * head_dim)
        yh = y[:, sl]
        ss = jnp.sum(yh * yh, axis=-1, keepdims=True)
        r = lax.rsqrt((ss / head_dim if mean else ss) + EPS)
        o_ref[:, col0 + hh * head_dim:col0 + (hh + 1) * head_dim] = (yh * r * gain).astype(o_ref.dtype)


def _qkv_kernel(x_ref, g_ref, w_ref, qkg_ref, o_ref, h_ref, *, steps_per_kind, head_dim):
    j = pl.program_id(1)

    @pl.when(j == 0)
    def _():
        h_ref[...] = _rms_rows(x_ref[...], g_ref[...]).astype(BF16)

    kind = j // steps_per_kind
    pieces = range(o_ref.shape[1] // SUB)

    @pl.when(kind < 2)
    def _():
        gain = qkg_ref[pl.ds(kind, 1), :]
        for t in pieces:
            y = _dot(h_ref[...], w_ref[:, t * SUB:(t + 1) * SUB])
            _head_norm_store(y, o_ref, t * SUB, head_dim, gain, mean=True)

    @pl.when(kind == 2)
    def _():
        for t in pieces:
            cols = slice(t * SUB, (t + 1) * SUB)
            o_ref[:, cols] = _dot(h_ref[...], w_ref[:, cols]).astype(o_ref.dtype)


def _qkv_proj(x, gain, w, qk_gain, head_dim):
    m, d = x.shape
    n = w.shape[1]
    kern = functools.partial(_qkv_kernel, steps_per_kind=d // TN, head_dim=head_dim)
    return pl.pallas_call(
        kern,
        grid=(m // TM, n // TN),
        in_specs=[
            pl.BlockSpec((TM, d), lambda i, j: (i, 0)),
            pl.BlockSpec((1, d), lambda i, j: (0, 0)),
            pl.BlockSpec((d, TN), lambda i, j: (0, j)),
            pl.BlockSpec((SUBLANES, head_dim), lambda i, j: (0, 0)),
        ],
        out_specs=pl.BlockSpec((TM, TN), lambda i, j: (i, j)),
        out_shape=jax.ShapeDtypeStruct((m, n), BF16),
        scratch_shapes=[pltpu.VMEM((TM, d), BF16)],
        compiler_params=_compiler_params(("arbitrary", "arbitrary"), 56 * 1024 * 1024),
        name="qkv_proj",
    )(x, gain, w, qk_gain)


def _ffn_kernel(x_ref, g_ref, wu_ref, wg_ref, cw_ref, wd_ref, o_ref, h_ref, tail_ref, ext_ref,
                *, tiles_per_seq):
    i = pl.program_id(0)
    j = pl.program_id(1)

    @pl.when(j == 0)
    def _():
        x = x_ref[...]
        h_ref[...] = _rms_rows(x, g_ref[...]).astype(BF16)
        o_ref[...] = x

    h = h_ref[...]
    u = _dot(h, wu_ref[...])
    gate = _dot(h, wg_ref[...])
    first = (i % tiles_per_seq) == 0
    c = _causal_conv_tile(u, cw_ref[...], tail_ref, ext_ref, j, first)
    act = (_silu(c) * gate).astype(BF16)
    o_ref[...] += _dot(act, wd_ref[...])


def _conv_ffn_call(x, gain, wu, wg, cw, wd, seq):
    m, d = x.shape
    n = wg.shape[1]
    kern = functools.partial(_ffn_kernel, tiles_per_seq=seq // TM)
    return pl.pallas_call(
        kern,
        grid=(m // TM, n // SUB),
        in_specs=[
            pl.BlockSpec((TM, d), lambda i, j: (i, 0)),
            pl.BlockSpec((1, d), lambda i, j: (0, 0)),
            pl.BlockSpec((d, SUB), lambda i, j: (0, j)),
            pl.BlockSpec((d, SUB), lambda i, j: (0, j)),
            pl.BlockSpec((cw.shape[0], SUB), lambda i, j: (0, j)),
            pl.BlockSpec((SUB, d), lambda i, j: (j, 0)),
        ],
        out_specs=pl.BlockSpec((TM, d), lambda i, j: (i, 0)),
        out_shape=jax.ShapeDtypeStruct((m, d), F32),
        scratch_shapes=[
            pltpu.VMEM((TM, d), BF16),
            pltpu.VMEM((n // SUB, SUBLANES, SUB), F32),
            pltpu.VMEM((TM + SUBLANES, SUB), F32),
        ],
        compiler_params=_compiler_params(("arbitrary", "arbitrary"), 56 * 1024 * 1024),
        name="conv_ffn",
    )(x, gain, wu, wg, cw, wd)


def _gdn_proj_kernel(x_ref, g_ref, w_ref, cw_ref, o_ref, h_ref, tail_ref, ext_ref,
                     *, tiles_per_seq, n_head_steps, n_conv_steps, head_dim):
    i = pl.program_id(0)
    j = pl.program_id(1)

    @pl.when(j == 0)
    def _():
        h_ref[...] = _rms_rows(x_ref[...], g_ref[...]).astype(BF16)

    n_pieces = o_ref.shape[1] // SUB
    first = (i % tiles_per_seq) == 0

    def conv_piece(t):
        cols = slice(t * SUB, (t + 1) * SUB)
        y = _dot(h_ref[...], w_ref[:, cols])
        return _silu(_causal_conv_tile(y, cw_ref[:, cols], tail_ref, ext_ref.at[t],
                                       j * n_pieces + t, first))

    @pl.when(j < 2 * n_head_steps)
    def _():
        scale = jnp.where(j < n_head_steps, head_dim ** -0.5, 1.0)
        for t in range(n_pieces):
            _head_norm_store(conv_piece(t), o_ref, t * SUB, head_dim, scale, mean=False)

    @pl.when((j >= 2 * n_head_steps) & (j < n_conv_steps))
    def _():
        for t in range(n_pieces):
            o_ref[:, t * SUB:(t + 1) * SUB] = conv_piece(t).astype(o_ref.dtype)

    @pl.when(j >= n_conv_steps)
    def _():
        for t in range(n_pieces):
            cols = slice(t * SUB, (t + 1) * SUB)
            o_ref[:, cols] = _dot(h_ref[...], w_ref[:, cols]).astype(o_ref.dtype)


def _gdn_proj(x, gain, w, cw, seq, key_dim, head_dim):
    m, d = x.shape
    n = w.shape[1]
    n_conv_steps = cw.shape[1] // TN
    n_pieces = TN // SUB
    kern = functools.partial(
        _gdn_proj_kernel, tiles_per_seq=seq // TM, n_head_steps=key_dim // TN,
        n_conv_steps=n_conv_steps, head_dim=head_dim)
    return pl.pallas_call(
        kern,
        grid=(m // TM, n // TN),
        in_specs=[
            pl.BlockSpec((TM, d), lambda i, j: (i, 0)),
            pl.BlockSpec((1, d), lambda i, j: (0, 0)),
            pl.BlockSpec((d, TN), lambda i, j: (0, j)),
            pl.BlockSpec((cw.shape[0], TN), lambda i, j: (0, jnp.minimum(j, n_conv_steps - 1))),
        ],
        out_specs=pl.BlockSpec((TM, TN), lambda i, j: (i, j)),
        out_shape=jax.ShapeDtypeStruct((m, n), BF16),
        scratch_shapes=[
            pltpu.VMEM((TM, d), BF16),
            pltpu.VMEM((n_conv_steps * n_pieces, SUBLANES, SUB), F32),
            pltpu.VMEM((n_pieces, TM + SUBLANES, SUB), F32),
        ],
        compiler_params=_compiler_params(("arbitrary", "arbitrary"), 58 * 1024 * 1024),
        name="gdn_proj",
    )(x, gain, w, cw)


def _gdn_gate_kernel(x_ref, g_ref, w_ref, alog_ref, dtb_ref, o_ref, *, n_heads):
    h = _rms_rows(x_ref[...], g_ref[...]).astype(BF16)
    y = _dot(h, w_ref[...])
    is_decay = lax.broadcasted_iota(jnp.int32, (1, y.shape[1]), 1) < n_heads
    g = -jnp.exp(alog_ref[...]) * jax.nn.softplus(y + dtb_ref[...])
    vals = jnp.where(is_decay, g, jax.nn.sigmoid(y))
    ri = lax.broadcasted_iota(jnp.int32, (CHUNK, CHUNK), 0)
    ci = lax.broadcasted_iota(jnp.int32, (CHUNK, CHUNK), 1)
    tri = (ri >= ci).astype(F32)
    for c in range(y.shape[0] // CHUNK):
        rows = slice(c * CHUNK, (c + 1) * CHUNK)
        v = vals[rows, :]
        cum = jnp.dot(tri, v, precision=lax.Precision.HIGHEST, preferred_element_type=F32)
        o_ref[rows, :] = jnp.where(is_decay, cum, v)


def _gdn_gates(x, gain, w_ab, alog, dtb, n_heads):
    m, d = x.shape
    kern = functools.partial(_gdn_gate_kernel, n_heads=n_heads)
    return pl.pallas_call(
        kern,
        grid=(m // TM,),
        in_specs=[
            pl.BlockSpec((TM, d), lambda i: (i, 0)),
            pl.BlockSpec((1, d), lambda i: (0, 0)),
            pl.BlockSpec((d, LANES), lambda i: (0, 0)),
            pl.BlockSpec((1, LANES), lambda i: (0, 0)),
            pl.BlockSpec((1, LANES), lambda i: (0, 0)),
        ],
        out_specs=pl.BlockSpec((TM, LANES), lambda i: (i, 0)),
        out_shape=jax.ShapeDtypeStruct((m, LANES), F32),
        compiler_params=_compiler_params(("arbitrary",), 32 * 1024 * 1024),
        name="gdn_gates",
    )(x, gain, w_ab, alog, dtb)


def _mm_resid_kernel(a_ref, w_ref, r_ref, o_ref):
    o_ref[...] = r_ref[...] + _dot(a_ref[...], w_ref[...])


def _mm_resid(a, w, resid):
    m, k = a.shape
    n = w.shape[1]
    tm = TM // 2
    return pl.pallas_call(
        _mm_resid_kernel,
        grid=(m // tm,),
        in_specs=[
            pl.BlockSpec((tm, k), lambda i: (i, 0)),
            pl.BlockSpec((k, n), lambda i: (0, 0), pipeline_mode=pl.Buffered(1)),
            pl.BlockSpec((tm, n), lambda i: (i, 0)),
        ],
        out_specs=pl.BlockSpec((tm, n), lambda i: (i, 0)),
        out_shape=jax.ShapeDtypeStruct((m, n), F32),
        compiler_params=_compiler_params(("arbitrary",), 48 * 1024 * 1024),
        name="mm_resid",
    )(a, w, resid)


def _bias_table_kernel(rb_ref, idx_ref, add_ref, o_ref):
    nrel = rb_ref.shape[1]
    rb = rb_ref[...]
    hi = rb.astype(BF16)
    rest = rb - hi.astype(F32)
    mid = rest.astype(BF16)
    lo = (rest - mid.astype(F32)).astype(BF16)
    rel = lax.broadcasted_iota(jnp.int32, (nrel, idx_ref.shape[1]), 0)
    for r in range(idx_ref.shape[0]):
        onehot = jnp.where(rel == idx_ref[r:r + 1, :], 1.0, 0.0).astype(BF16)
        picked = _dot(hi, onehot) + _dot(mid, onehot) + _dot(lo, onehot)
        o_ref[:, r, :] = picked * LOG2E + add_ref[r:r + 1, :]


def _band_layout(rows, keys):
    qpos = np.arange(rows)[:, None]
    kpos = np.arange(keys)[None, :] - (keys - rows)
    qc = qpos // CHUNK
    kc = np.floor_divide(kpos, CHUNK)
    in_band = (kc <= qc) & (kc >= qc - LEFT_CHUNKS)
    idx = np.clip(qpos - kpos, -(CHUNK - 1), MAX_REL) + (CHUNK - 1)
    idx = np.where(in_band, idx, 0).astype(np.int32)
    add = np.where(in_band, 0.0, MASK_VALUE).astype(np.float32)
    return idx, add


def _bias_table(rel_bias, rows, keys):
    heads, nrel = rel_bias.shape
    nrel_pad = -(-nrel // LANES) * LANES
    rb = jnp.pad(rel_bias.astype(F32), ((0, 0), (0, nrel_pad - nrel)))
    idx, add = _band_layout(rows, keys)
    tr = SUBLANES
    return pl.pallas_call(
        _bias_table_kernel,
        grid=(rows // tr,),
        in_specs=[
            pl.BlockSpec((heads, nrel_pad), lambda i: (0, 0)),
            pl.BlockSpec((tr, keys), lambda i: (i, 0)),
            pl.BlockSpec((tr, keys), lambda i: (i, 0)),
        ],
        out_specs=pl.BlockSpec((heads, tr, keys), lambda i: (0, i, 0)),
        out_shape=jax.ShapeDtypeStruct((heads, rows, keys), F32),
        compiler_params=_compiler_params(("arbitrary",), 32 * 1024 * 1024),
        name="att_bias_table",
    )(rb, jnp.asarray(idx), jnp.asarray(add))


def _attn_kernel(q_ref, k0_ref, k1_ref, k2_ref, v0_ref, v1_ref, v2_ref, bias_ref, o_ref,
                 *, head_dim):
    qb = pl.program_id(2)
    rows = q_ref.shape[0]
    k_refs = (k0_ref, k1_ref, k2_ref)
    v_refs = (v0_ref, v1_ref, v2_ref)
    nblk = len(k_refs)
    heads = range(q_ref.shape[1] // head_dim)
    sl = [slice(hh * head_dim, (hh + 1) * head_dim) for hh in heads]
    def qk(hh):
        q = q_ref[:, sl[hh]]
        row = []
        for t in range(nblk):
            s = _dot_nt(q, k_refs[t][:, sl[hh]]) + bias_ref[hh, :, t * rows:(t + 1) * rows]
            if t < nblk - 1:
                s = jnp.where(qb >= nblk - 1 - t, s, MASK_VALUE)
            row.append(s)
        return row

    def fold(parts, op):
        m = functools.reduce(op, parts)
        return functools.reduce(op, [m[:, i:i + LANES] for i in range(0, m.shape[1], LANES)])

    ones = jnp.ones((rows, head_dim), BF16)

    def softmax(sc):
        mx = fold(sc, jnp.maximum).max(axis=-1, keepdims=True)
        return [jnp.exp2((s - mx).astype(BF16)) for s in sc]

    def pv(hh, p):
        acc = sum(_dot(p[t], jnp.concatenate([v_refs[t][:, sl[hh]], ones], axis=1))
                  for t in range(nblk))
        o_ref[:, sl[hh]] = (acc[:, :head_dim] / acc[:, head_dim:]).astype(o_ref.dtype)

    sc = qk(0)
    for hh in heads:
        nxt = qk(hh + 1) if hh + 1 < len(heads) else None
        pv(hh, softmax(sc))
        sc = nxt


def _band_attention(qkv, bias, layer, batch, seq, d, head_dim):
    m = qkv.shape[0]
    r = ATT_ROWS
    gw = ATT_HEAD_GROUP * head_dim
    n_groups = d // gw
    nqb = seq // r
    nblk = bias.shape[2] // r

    def q_map(g, b, qb):
        return (b * nqb + qb, g)

    def kv_map(off, col0):
        def f(g, b, qb):
            return (b * nqb + jnp.maximum(qb - off, 0), col0 + g)
        return f

    k_specs = [pl.BlockSpec((r, gw), kv_map(nblk - 1 - t, n_groups)) for t in range(nblk)]
    v_specs = [pl.BlockSpec((r, gw), kv_map(nblk - 1 - t, 2 * n_groups)) for t in range(nblk)]
    kern = functools.partial(_attn_kernel, head_dim=head_dim)
    return pl.pallas_call(
        kern,
        grid=(n_groups, batch, nqb),
        in_specs=[pl.BlockSpec((r, gw), q_map)] + k_specs + v_specs + [
            pl.BlockSpec((ATT_HEAD_GROUP, r, nblk * r),
                         lambda g, b, qb: (layer * n_groups + g, 0, 0),
                         pipeline_mode=pl.Buffered(1))],
        out_specs=pl.BlockSpec((r, gw), q_map),
        out_shape=jax.ShapeDtypeStruct((m, d), BF16),
        compiler_params=_compiler_params(("arbitrary", "arbitrary", "arbitrary"),
                                         48 * 1024 * 1024),
        name="band_attention",
    )(qkv, qkv, qkv, qkv, qkv, qkv, qkv, bias)


def _pool_kernel(x_ref, g_ref, w_ref, sc_ref, o_ref, ext_ref, lvl_ref, *, tiles_per_seq, windows):
    i = pl.program_id(0)
    tm, d = x_ref.shape
    halo = POOL_HALO
    n = halo + tm
    x = x_ref[...]
    h = _rms_rows(x, g_ref[...])
    first = (i % tiles_per_seq) == 0

    @pl.when(first)
    def _():
        ext_ref[0:halo, :] = jnp.zeros((halo, d), F32)

    @pl.when(jnp.logical_not(first))
    def _():
        ext_ref[0:halo, :] = ext_ref[tm:tm + halo, :]

    ext_ref[halo:halo + tm, :] = h
    pos = (i % tiles_per_seq) * tm + lax.broadcasted_iota(jnp.int32, (tm, 1), 0)
    dg = d // len(windows)
    for g, w in enumerate(windows):
        cs = slice(g * dg, (g + 1) * dg)
        levels = w.bit_length() - 1
        assert w == 1 << levels and SUBLANES * levels <= halo
        src = ext_ref.at[:, cs]
        for k in range(levels):
            lo = SUBLANES * (k + 1)
            shift = 1 << k
            summed = src[lo:n, :] + src[pl.ds(lo - shift, n - lo), :]
            if k + 1 < levels:
                dst = lvl_ref.at[k % 2]
                dst[lo:n, :] = summed
                src = dst
        acc = summed[halo - SUBLANES * levels:, :]
        hg = h[:, cs]
        cnt = jnp.minimum(pos + 1, w).astype(F32)
        pooled = acc / cnt - hg
        y = _dot(pooled.astype(BF16), w_ref[g]) * sc_ref[:, cs]
        o_ref[:, cs] = x[:, cs] + y


def _pool_mixer(x, gain, pool_w, pool_scale, seq):
    m, d = x.shape
    g, dg, _ = pool_w.shape
    halo = POOL_HALO
    kern = functools.partial(_pool_kernel, tiles_per_seq=seq // POOL_TM, windows=POOL_WINDOWS)
    return pl.pallas_call(
        kern,
        grid=(m // POOL_TM,),
        in_specs=[
            pl.BlockSpec((POOL_TM, d), lambda i: (i, 0)),
            pl.BlockSpec((1, d), lambda i: (0, 0)),
            pl.BlockSpec((g, dg, dg), lambda i: (0, 0, 0)),
            pl.BlockSpec((1, d), lambda i: (0, 0)),
        ],
        out_specs=pl.BlockSpec((POOL_TM, d), lambda i: (i, 0)),
        out_shape=jax.ShapeDtypeStruct((m, d), F32),
        scratch_shapes=[pltpu.VMEM((POOL_TM + halo, d), F32),
                        pltpu.VMEM((2, POOL_TM + halo, dg), F32)],
        compiler_params=_compiler_params(("arbitrary",), 40 * 1024 * 1024),
        name="pool_mixer",
    )(x, gain, pool_w, pool_scale)


def _mm_inv(p, q):
    return _dot(p.astype(BF16), q.astype(BF16))


def _gdn_kernel(q_ref, k_ref, v_ref, gb_ref, rows_ref, gate_ref, srows_ref, og_ref, o_ref,
                s_ref, gq_ref, ho_ref, *, n_steps, n_kheads, n_blocks, n_chunks, n_heads, dv):
    t = pl.program_id(0)
    slot_p = t % 2
    slot_s = 1 - slot_p
    hk = (jnp.minimum(t, n_steps - 1) // n_blocks) % n_kheads
    scan_block = jnp.maximum(t - 1, 0) % n_blocks

    c2 = 2 * CHUNK
    blk = c2 + CHUNK
    ri = lax.broadcasted_iota(jnp.int32, (c2, c2), 0)
    ci = lax.broadcasted_iota(jnp.int32, (c2, c2), 1)
    same = (ri // CHUNK) == (ci // CHUNK)
    causal = same & (ri >= ci)
    strict = same & (ri > ci)
    eye = (ri == ci).astype(F32)
    head0_lanes = ci < CHUNK
    lane = lax.broadcasted_iota(jnp.int32, (CHUNK, LANES), 1)
    og = og_ref[...]
    n = range(n_chunks)

    def col(gb, idx):
        return jnp.sum(jnp.where(lane == idx, gb, 0.0), axis=1, keepdims=True)

    def prepare():
        r0 = [c * CHUNK for c in n]
        q = [q_ref[r:r + CHUNK, :] for r in r0]
        k = [k_ref[r:r + CHUNK, :] for r in r0]
        gb = [gb_ref[r:r + CHUNK, :] for r in r0]
        rows = [rows_ref[0, c] for c in n]
        grow = [x[0:1] for x in rows]
        brow = [x[1:2] for x in rows]
        glast = [x[2:3] for x in rows]
        gcol = [jnp.concatenate([col(g, 2 * hk), col(g, 2 * hk + 1)], axis=0) for g in gb]
        bcol = [jnp.concatenate([col(g, n_heads + 2 * hk), col(g, n_heads + 2 * hk + 1)], axis=0)
                for g in gb]
        yield
        k2 = [jnp.concatenate([x, x], axis=0) for x in k]
        qkk = [_dot_nt(jnp.concatenate([q[i], k[i]], axis=0), k2[i]) for i in n]
        yield
        qk = [jnp.concatenate([x[:CHUNK], x[:CHUNK]], axis=0) for x in qkk]
        kk = [jnp.concatenate([x[CHUNK:], x[CHUNK:]], axis=0) for x in qkk]
        decay = [jnp.where(causal, jnp.exp(jnp.where(causal, gcol[i] - grow[i], 0.0)), 0.0)
                 for i in n]
        a = [jnp.where(strict, bcol[i] * kk[i] * decay[i], 0.0) for i in n]
        attn = [qk[i] * decay[i] for i in n]
        yield
        base = SUBLANES
        bmask = (ri // base) == (ci // base)
        a0 = [jnp.where(bmask, m, 0.0) for m in a]
        p = [_mm_inv(m, m) for m in a0]
        x = [eye - m for m in a0]
        yield
        x = [xi + _mm_inv(xi, pi) for xi, pi in zip(x, p)]
        yield
        p = [_mm_inv(pi, pi) for pi in p]
        yield
        x = [xi + _mm_inv(xi, pi) for xi, pi in zip(x, p)]
        yield
        s = base
        while s < CHUNK:
            nmask = ((ri // (2 * s)) == (ci // (2 * s))) & ((ri // s) != (ci // s))
            y = [_mm_inv(jnp.where(nmask, ai, 0.0), xi) for ai, xi in zip(a, x)]
            yield
            x = [xi - _mm_inv(xi, yi) for xi, yi in zip(x, y)]
            yield
            s *= 2
        v = [v_ref[r:r + CHUNK, :] for r in r0]
        v2 = [jnp.concatenate([m[:, :dv], m[:, dv:]], axis=0) for m in v]
        u = [_dot((x[i] * brow[i]).astype(BF16), v2[i]) for i in n]
        yield
        w = [_dot((x[i] * (brow[i] * jnp.exp(grow[i]))).astype(BF16), k2[i]) for i in n]
        yield
        wu = [jnp.concatenate([w[i], u[i]], axis=1).astype(BF16) for i in n]
        kte = [k2[i].astype(F32).T * jnp.exp(glast[i] - grow[i]) for i in n]
        lhs = [jnp.concatenate([
            jnp.where(head0_lanes, kte[i], 0.0),
            jnp.where(head0_lanes, 0.0, kte[i]),
            attn[i]], axis=0).astype(BF16) for i in n]
        yield
        r = [_dot(lhs[i], wu[i]) for i in n]
        yield
        for i in n:
            q2f = jnp.concatenate([q[i], q[i]], axis=0).astype(F32)
            qmat = q2f * jnp.exp(gcol[i]) - r[i][2 * c2:, :dv]
            omat = r[i][2 * c2:, dv:]
            gq_ref[slot_p, i] = jnp.concatenate([
                -r[i][:c2, :dv], qmat[:CHUNK], -r[i][c2:2 * c2, :dv], qmat[CHUNK:]],
                axis=0).astype(BF16)
            ho_ref[slot_p, i] = jnp.concatenate([
                r[i][:c2, dv:], omat[:CHUNK], r[i][c2:2 * c2, dv:], omat[CHUNK:]], axis=0)
        yield

    def scan():
        fresh = scan_block == 0
        states = [jnp.where(fresh, 0.0, s_ref[j]) for j in range(2)]
        for c in n:
            rows = srows_ref[0, c]
            gate = gate_ref[c * CHUNK:(c + 1) * CHUNK, :]
            outs = []
            for j in range(2):
                s = states[j]
                r = _dot(gq_ref[slot_s, c, j * blk:(j + 1) * blk, :], s.astype(BF16))
                ho = ho_ref[slot_s, c, j * blk:(j + 1) * blk, :]
                dec = jnp.exp(rows[3 + j:4 + j])
                states[j] = s * dec + r[:c2] + ho[:c2]
                o = r[c2:] + ho[c2:]
                on = o * lax.rsqrt(jnp.mean(o * o, axis=-1, keepdims=True) + EPS) * og
                outs.append(on * _silu(gate[:, j * dv:(j + 1) * dv].astype(F32)))
            o_ref[c * CHUNK:(c + 1) * CHUNK, :] = jnp.concatenate(outs, axis=1).astype(o_ref.dtype)
            yield
        s_ref[0] = states[0]
        s_ref[1] = states[1]
        yield

    def run(generators, weights):
        live = list(generators)
        while any(g is not None for g in live):
            for i, g in enumerate(live):
                for _ in range(weights[i]):
                    if live[i] is not None and next(live[i], "done") == "done":
                        live[i] = None

    @pl.when(t == 0)
    def _():
        s_ref[...] = jnp.zeros(s_ref.shape, F32)
        run([prepare()], [1])

    @pl.when((t > 0) & (t < n_steps))
    def _():
        run([scan(), prepare()], [1, 1])

    @pl.when(t == n_steps)
    def _():
        run([scan()], [1])


def _gdn_rows(gb, n_heads, n_kheads):
    m = gb.shape[0]
    nc = m // CHUNK
    rep = n_heads // n_kheads
    assert rep * CHUNK == LANES

    def stack(x):
        x = x.reshape(nc, CHUNK, n_kheads, rep).transpose(2, 0, 3, 1)
        return x.reshape(n_kheads, nc, rep * CHUNK)

    gc = gb[:, :n_heads].reshape(nc, CHUNK, n_heads)
    beta = gb[:, n_heads:2 * n_heads].reshape(nc, CHUNK, n_heads)
    last = jnp.broadcast_to(gc[:, CHUNK - 1:, :], gc.shape)
    last_k = gc[:, CHUNK - 1, :].reshape(nc, n_kheads, rep).transpose(1, 0, 2)
    per_head = [jnp.broadcast_to(last_k[:, :, r:r + 1], (n_kheads, nc, LANES)) for r in range(rep)]
    rows = [stack(gc), stack(beta), stack(last)] + per_head
    rows += [jnp.zeros_like(rows[0])] * (SUBLANES - len(rows))
    return jnp.stack(rows, axis=2)


def _gated_delta(proj, gb, rows, o_gain, batch, seq, n_kheads, n_heads, dk, dv):
    m = proj.shape[0]
    cb = GDN_CHUNKS_PER_STEP
    rb = cb * CHUNK
    ncb = seq // rb
    key_dim = n_kheads * dk
    val_dim = n_heads * dv
    vw = 2 * dv
    n_steps = batch * n_kheads * ncb
    assert n_heads == 2 * n_kheads and dk == LANES and dv == LANES

    def decode(idx):
        return (idx // (n_kheads * ncb)) * ncb + idx % ncb, (idx // ncb) % n_kheads

    def prep(col0):
        def f(t):
            row, hk = decode(jnp.minimum(t, n_steps - 1))
            return row, col0 + hk
        return f

    def lagged(col0):
        def f(t):
            row, hk = decode(jnp.maximum(t - 1, 0))
            return row, col0 + hk
        return f

    def rows_map(f):
        def g(t):
            row, hk = f(t)
            return hk, row, 0, 0
        return g

    kern = functools.partial(_gdn_kernel, n_steps=n_steps, n_kheads=n_kheads, n_blocks=ncb,
                             n_chunks=cb, n_heads=n_heads, dv=dv)
    return pl.pallas_call(
        kern,
        grid=(n_steps + 1,),
        in_specs=[
            pl.BlockSpec((rb, dk), prep(0)),
            pl.BlockSpec((rb, dk), prep(key_dim // dk)),
            pl.BlockSpec((rb, vw), prep(2 * key_dim // vw)),
            pl.BlockSpec((rb, LANES), lambda t: (prep(0)(t)[0], 0)),
            pl.BlockSpec((1, cb, SUBLANES, LANES), rows_map(prep(0))),
            pl.BlockSpec((rb, vw), lagged((2 * key_dim + val_dim) // vw)),
            pl.BlockSpec((1, cb, SUBLANES, LANES), rows_map(lagged(0))),
            pl.BlockSpec((1, dv), lambda t: (0, 0)),
        ],
        out_specs=pl.BlockSpec((rb, vw), lagged(0)),
        out_shape=jax.ShapeDtypeStruct((m, val_dim), BF16),
        scratch_shapes=[
            pltpu.VMEM((2, dk, dv), F32),
            pltpu.VMEM((2, cb, 2 * (dk + CHUNK), dv), BF16),
            pltpu.VMEM((2, cb, 2 * (dk + CHUNK), dv), F32),
        ],
        compiler_params=_compiler_params(("arbitrary",), 32 * 1024 * 1024),
        name="gated_delta",
    )(proj, proj, proj, gb, rows, proj, rows, o_gain)


def _pad_cols(w, n):
    return jnp.pad(w, ((0, 0), (0, n - w.shape[1])))


def _attention_layer(x, gain, w_qkv, q_gain, k_gain, bias, layer, w_o, batch, seq):
    d = x.shape[1]
    head_dim = d // ATT_HEADS
    qk_gain = jnp.zeros((SUBLANES, head_dim), F32)
    qk_gain = qk_gain.at[0].set(q_gain.astype(F32) * (head_dim ** -0.5 * LOG2E)).at[1].set(k_gain.astype(F32))
    qkv = _qkv_proj(x, gain, w_qkv.astype(BF16), qk_gain, head_dim)
    o = _band_attention(qkv, bias, layer, batch, seq, d, head_dim)
    return _mm_resid(o, w_o.astype(BF16), x)


def _gdn_layer(x, gain, w_in, conv_w, a_log, dt_bias, o_gain, w_o, batch, seq):
    d = x.shape[1]
    dk = d // GDN_K_HEADS
    dv = dk
    key_dim = GDN_K_HEADS * dk
    val_dim = GDN_V_HEADS * dv
    conv_ch = 2 * key_dim + val_dim
    n_main = conv_ch + val_dim
    w_in = w_in.astype(BF16)
    proj = _gdn_proj(x, gain, w_in[:, :n_main], conv_w.astype(F32), seq, key_dim, dk)
    w_ab = _pad_cols(w_in[:, n_main:], LANES)
    pad = LANES - GDN_V_HEADS
    alog = jnp.pad(a_log.astype(F32), (0, pad)).reshape(1, LANES)
    dtb = jnp.pad(dt_bias.astype(F32), (0, pad)).reshape(1, LANES)
    gb = _gdn_gates(x, gain, w_ab, alog, dtb, GDN_V_HEADS)
    rows = _gdn_rows(gb, GDN_V_HEADS, GDN_K_HEADS)
    o = _gated_delta(proj, gb, rows, o_gain.astype(F32).reshape(1, dv), batch, seq,
                     GDN_K_HEADS, GDN_V_HEADS, dk, dv)
    return _mm_resid(o, w_o.astype(BF16), x)


def _conv_ffn(x, gain, w_up, conv_w, w_down, seq):
    d_ff = conv_w.shape[1]
    d_ff_pad = -(-d_ff // SUB) * SUB
    w_up = w_up.astype(BF16)
    wu = _pad_cols(w_up[:, :d_ff], d_ff_pad)
    wg = _pad_cols(w_up[:, d_ff:], d_ff_pad)
    cw = _pad_cols(conv_w.astype(F32), d_ff_pad)
    wd = jnp.pad(w_down.astype(BF16), ((0, d_ff_pad - d_ff), (0, 0)))
    return _conv_ffn_call(x, gain, wu, wg, cw, wd, seq)


def kernel(x, mix_norm, ffn_norm, att_w_qkv, att_q_gain, att_k_gain, att_rel_bias, att_w_o,
           pool_w, pool_scale, gdn_w_in, gdn_conv, gdn_a_log, gdn_dt_bias, gdn_o_gain,
           gdn_w_o, ffn_w_up, ffn_conv, ffn_w_down):
    batch, seq, d = x.shape
    depth = mix_norm.shape[0]
    assert seq % TM == 0 and seq % POOL_TM == 0 and seq % ATT_ROWS == 0
    assert seq % (GDN_CHUNKS_PER_STEP * CHUNK) == 0
    xf = x.reshape(batch * seq, d).astype(F32)
    att_bias = _bias_table(att_rel_bias.reshape(-1, att_rel_bias.shape[-1]), ATT_ROWS,
                           ATT_ROWS + LEFT_CHUNKS * CHUNK)
    for i in range(depth):
        kind = i % N_MIXERS
        j = i // N_MIXERS
        gain = mix_norm[i].astype(F32).reshape(1, d)
        if kind == 0:
            xf = _attention_layer(xf, gain, att_w_qkv[j], att_q_gain[j], att_k_gain[j],
                                  att_bias, j, att_w_o[j], batch, seq)
        elif kind == 1:
            xf = _pool_mixer(xf, gain, pool_w[j].astype(BF16),
                             pool_scale[j].astype(F32).reshape(1, d), seq)
        else:
            xf = _gdn_layer(xf, gain, gdn_w_in[j], gdn_conv[j], gdn_a_log[j], gdn_dt_bias[j],
                            gdn_o_gain[j], gdn_w_o[j], batch, seq)
        fgain = ffn_norm[i].astype(F32).reshape(1, d)
        xf = _conv_ffn(xf, fgain, ffn_w_up[i], ffn_conv[i], ffn_w_down[i], seq)
    return xf.reshape(batch, seq, d).astype(x.dtype)
```

```python
import functools

import numpy as np
import jax
import jax.numpy as jnp
from jax import lax
from jax.experimental import pallas as pl
from jax.experimental.pallas import tpu as pltpu

F32 = jnp.float32
BF16 = jnp.bfloat16

CHUNK = 64
EPS = 1e-6
MASK_VALUE = -1e30
LOG2E = 1.4426950408889634
N_MIXERS = 3

ATT_HEADS = 16
LEFT_CHUNKS = 8
MAX_REL = 256

POOL_WINDOWS = (2, 4, 8, 16)

GDN_K_HEADS = 16
GDN_V_HEADS = 32

LANES = 128
SUBLANES = 8
V7X_VMEM_BYTES = 64 * 1024 * 1024

TM = 1024
TN = 2048
SUB = 512
POOL_TM = 512
POOL_HALO = 32
ATT_ROWS = 256
ATT_HEAD_GROUP = 16
GDN_CHUNKS_PER_STEP = 16
GDN_SELECT_EVERY = 4


def _compiler_params(semantics, vmem_bytes):
    assert vmem_bytes < V7X_VMEM_BYTES
    return pltpu.CompilerParams(dimension_semantics=semantics, vmem_limit_bytes=vmem_bytes)


def _rms_rows(x, gain):
    ms = jnp.mean(x * x, axis=-1, keepdims=True)
    return x * lax.rsqrt(ms + EPS) * gain


def _silu(x):
    return x * jax.nn.sigmoid(x)


def _dot(a, b):
    return jnp.dot(a, b, preferred_element_type=F32)


def _dot_nt(a, b):
    return lax.dot_general(a, b, (((1,), (1,)), ((), ())), preferred_element_type=F32)


def _causal_conv_tile(u, cw, tail_ref, ext_ref, t, first):
    tm = u.shape[0]
    k = cw.shape[0]
    prev = jnp.where(first, 0.0, tail_ref[t])
    ext_ref[0:SUBLANES, :] = prev
    ext_ref[SUBLANES:SUBLANES + tm, :] = u
    tail_ref[t] = u[tm - SUBLANES:, :]
    acc = u * cw[k - 1:k, :]
    for s in range(1, k):
        acc = acc + ext_ref[pl.ds(SUBLANES - s, tm), :] * cw[k - 1 - s:k - s, :]
    return acc


def _causal_conv_rolled(u, cw, tail_ref, t, first):
    tm = u.shape[0]
    k = cw.shape[0]
    prev = jnp.where(first, 0.0, tail_ref[t])
    tail_ref[t] = u[tm - SUBLANES:, :]
    row = lax.broadcasted_iota(jnp.int32, (SUBLANES, 1), 0)
    acc = u * cw[k - 1:k, :]
    for s in range(1, k):
        rolled = pltpu.roll(u, s, 0)
        head = jnp.where(row < s, pltpu.roll(prev, s, 0), rolled[:SUBLANES, :])
        acc = acc + jnp.concatenate([head, rolled[SUBLANES:, :]], axis=0) * cw[k - 1 - s:k - s, :]
    return acc


def _head_norm_store(y, o_ref, col0, head_dim, gain, mean):
    for hh in range(y.shape[1] // head_dim):
        sl = slice(hh * head_dim, (hh + 1) * head_dim)
        yh = y[:, sl]
        ss = jnp.sum(yh * yh, axis=-1, keepdims=True)
        r = lax.rsqrt((ss / head_dim if mean else ss) + EPS)
        o_ref[:, col0 + hh * head_dim:col0 + (hh + 1) * head_dim] = (yh * r * gain).astype(o_ref.dtype)


def _qkv_kernel(x_ref, g_ref, w_ref, qkg_ref, o_ref, h_ref, *, steps_per_kind, head_dim):
    j = pl.program_id(1)

    @pl.when(j == 0)
    def _():
        h_ref[...] = _rms_rows(x_ref[...], g_ref[...]).astype(BF16)

    kind = j // steps_per_kind
    pieces = range(o_ref.shape[1] // SUB)

    @pl.when(kind < 2)
    def _():
        gain = qkg_ref[pl.ds(kind, 1), :]
        for t in pieces:
            y = _dot(h_ref[...], w_ref[:, t * SUB:(t + 1) * SUB])
            _head_norm_store(y, o_ref, t * SUB, head_dim, gain, mean=True)

    @pl.when(kind == 2)
    def _():
        for t in pieces:
            cols = slice(t * SUB, (t + 1) * SUB)
            o_ref[:, cols] = _dot(h_ref[...], w_ref[:, cols]).astype(o_ref.dtype)


def _qkv_proj(x, gain, w, qk_gain, head_dim):
    m, d = x.shape
    n = w.shape[1]
    kern = functools.partial(_qkv_kernel, steps_per_kind=d // TN, head_dim=head_dim)
    return pl.pallas_call(
        kern,
        grid=(m // TM, n // TN),
        in_specs=[
            pl.BlockSpec((TM, d), lambda i, j: (i, 0)),
            pl.BlockSpec((1, d), lambda i, j: (0, 0)),
            pl.BlockSpec((d, TN), lambda i, j: (0, j)),
            pl.BlockSpec((SUBLANES, head_dim), lambda i, j: (0, 0)),
        ],
        out_specs=pl.BlockSpec((TM, TN), lambda i, j: (i, j)),
        out_shape=jax.ShapeDtypeStruct((m, n), BF16),
        scratch_shapes=[pltpu.VMEM((TM, d), BF16)],
        compiler_params=_compiler_params(("arbitrary", "arbitrary"), 56 * 1024 * 1024),
        name="qkv_proj",
    )(x, gain, w, qk_gain)


def _ffn_kernel(x_ref, g_ref, wu_ref, wg_ref, cw_ref, wd_ref, o_ref, h_ref, tail_ref,
                *, tiles_per_seq):
    i = pl.program_id(0)
    j = pl.program_id(1)

    @pl.when(j == 0)
    def _():
        x = x_ref[...]
        h_ref[...] = _rms_rows(x, g_ref[...]).astype(BF16)
        o_ref[...] = x

    h = h_ref[...]
    u = _dot(h, wu_ref[...])
    gate = _dot(h, wg_ref[...])
    first = (i % tiles_per_seq) == 0
    c = _causal_conv_rolled(u, cw_ref[...], tail_ref, j, first)
    act = (_silu(c) * gate).astype(BF16)
    o_ref[...] += _dot(act, wd_ref[...])


def _conv_ffn_call(x, gain, wu, wg, cw, wd, layer, seq):
    m, d = x.shape
    n = wg.shape[2]
    kern = functools.partial(_ffn_kernel, tiles_per_seq=seq // TM)
    return pl.pallas_call(
        kern,
        grid=(m // TM, n // SUB),
        in_specs=[
            pl.BlockSpec((TM, d), lambda i, j: (i, 0)),
            pl.BlockSpec((1, d), lambda i, j: (0, 0)),
            pl.BlockSpec((None, d, SUB), lambda i, j: (layer, 0, j)),
            pl.BlockSpec((None, d, SUB), lambda i, j: (layer, 0, j)),
            pl.BlockSpec((None, cw.shape[1], SUB), lambda i, j: (layer, 0, j)),
            pl.BlockSpec((None, SUB, d), lambda i, j: (layer, j, 0)),
        ],
        out_specs=pl.BlockSpec((TM, d), lambda i, j: (i, 0)),
        out_shape=jax.ShapeDtypeStruct((m, d), F32),
        scratch_shapes=[
            pltpu.VMEM((TM, d), BF16),
            pltpu.VMEM((n // SUB, SUBLANES, SUB), F32),
        ],
        compiler_params=_compiler_params(("arbitrary", "arbitrary"), 56 * 1024 * 1024),
        name="conv_ffn",
    )(x, gain, wu, wg, cw, wd)


def _gdn_proj_kernel(x_ref, g_ref, w_ref, cw_ref, o_ref, h_ref, tail_ref, ext_ref,
                     *, tiles_per_seq, n_head_steps, n_conv_steps, head_dim):
    i = pl.program_id(0)
    j = pl.program_id(1)

    @pl.when(j == 0)
    def _():
        h_ref[...] = _rms_rows(x_ref[...], g_ref[...]).astype(BF16)

    n_pieces = o_ref.shape[1] // SUB
    first = (i % tiles_per_seq) == 0

    def conv_piece(t):
        cols = slice(t * SUB, (t + 1) * SUB)
        y = _dot(h_ref[...], w_ref[:, cols])
        return _silu(_causal_conv_tile(y, cw_ref[:, cols], tail_ref, ext_ref.at[t],
                                       j * n_pieces + t, first))

    @pl.when(j < 2 * n_head_steps)
    def _():
        scale = jnp.where(j < n_head_steps, head_dim ** -0.5, 1.0)
        for t in range(n_pieces):
            _head_norm_store(conv_piece(t), o_ref, t * SUB, head_dim, scale, mean=False)

    @pl.when((j >= 2 * n_head_steps) & (j < n_conv_steps))
    def _():
        for t in range(n_pieces):
            o_ref[:, t * SUB:(t + 1) * SUB] = conv_piece(t).astype(o_ref.dtype)

    @pl.when(j >= n_conv_steps)
    def _():
        for t in range(n_pieces):
            cols = slice(t * SUB, (t + 1) * SUB)
            o_ref[:, cols] = _dot(h_ref[...], w_ref[:, cols]).astype(o_ref.dtype)


def _gdn_proj(x, gain, w, cw, seq, key_dim, head_dim):
    m, d = x.shape
    n = w.shape[1]
    n_conv_steps = cw.shape[1] // TN
    n_pieces = TN // SUB
    kern = functools.partial(
        _gdn_proj_kernel, tiles_per_seq=seq // TM, n_head_steps=key_dim // TN,
        n_conv_steps=n_conv_steps, head_dim=head_dim)
    return pl.pallas_call(
        kern,
        grid=(m // TM, n // TN),
        in_specs=[
            pl.BlockSpec((TM, d), lambda i, j: (i, 0)),
            pl.BlockSpec((1, d), lambda i, j: (0, 0)),
            pl.BlockSpec((d, TN), lambda i, j: (0, j)),
            pl.BlockSpec((cw.shape[0], TN), lambda i, j: (0, jnp.minimum(j, n_conv_steps - 1))),
        ],
        out_specs=pl.BlockSpec((TM, TN), lambda i, j: (i, j)),
        out_shape=jax.ShapeDtypeStruct((m, n), BF16),
        scratch_shapes=[
            pltpu.VMEM((TM, d), BF16),
            pltpu.VMEM((n_conv_steps * n_pieces, SUBLANES, SUB), F32),
            pltpu.VMEM((n_pieces, TM + SUBLANES, SUB), F32),
        ],
        compiler_params=_compiler_params(("arbitrary", "arbitrary"), 58 * 1024 * 1024),
        name="gdn_proj",
    )(x, gain, w, cw)


def _gdn_gate_kernel(x_ref, g_ref, w_ref, alog_ref, dtb_ref, o_ref, *, n_heads):
    h = _rms_rows(x_ref[...], g_ref[...]).astype(BF16)
    y = _dot(h, w_ref[...])
    is_decay = lax.broadcasted_iota(jnp.int32, (1, y.shape[1]), 1) < n_heads
    g = -jnp.exp(alog_ref[...]) * jax.nn.softplus(y + dtb_ref[...])
    vals = jnp.where(is_decay, g, jax.nn.sigmoid(y))
    ri = lax.broadcasted_iota(jnp.int32, (CHUNK, CHUNK), 0)
    ci = lax.broadcasted_iota(jnp.int32, (CHUNK, CHUNK), 1)
    tri = (ri >= ci).astype(F32)
    for c in range(y.shape[0] // CHUNK):
        rows = slice(c * CHUNK, (c + 1) * CHUNK)
        v = vals[rows, :]
        cum = jnp.dot(tri, v, precision=lax.Precision.HIGHEST, preferred_element_type=F32)
        o_ref[rows, :] = jnp.where(is_decay, cum, v)


def _gdn_gates(x, gain, w_ab, alog, dtb, n_heads):
    m, d = x.shape
    kern = functools.partial(_gdn_gate_kernel, n_heads=n_heads)
    return pl.pallas_call(
        kern,
        grid=(m // TM,),
        in_specs=[
            pl.BlockSpec((TM, d), lambda i: (i, 0)),
            pl.BlockSpec((1, d), lambda i: (0, 0)),
            pl.BlockSpec((d, LANES), lambda i: (0, 0)),
            pl.BlockSpec((1, LANES), lambda i: (0, 0)),
            pl.BlockSpec((1, LANES), lambda i: (0, 0)),
        ],
        out_specs=pl.BlockSpec((TM, LANES), lambda i: (i, 0)),
        out_shape=jax.ShapeDtypeStruct((m, LANES), F32),
        compiler_params=_compiler_params(("arbitrary",), 32 * 1024 * 1024),
        name="gdn_gates",
    )(x, gain, w_ab, alog, dtb)


def _mm_resid_kernel(a_ref, w_ref, r_ref, o_ref):
    o_ref[...] = r_ref[...] + _dot(a_ref[...], w_ref[...])


def _mm_resid(a, w, resid):
    m, k = a.shape
    n = w.shape[1]
    tm = TM // 2
    return pl.pallas_call(
        _mm_resid_kernel,
        grid=(m // tm,),
        in_specs=[
            pl.BlockSpec((tm, k), lambda i: (i, 0)),
            pl.BlockSpec((k, n), lambda i: (0, 0), pipeline_mode=pl.Buffered(1)),
            pl.BlockSpec((tm, n), lambda i: (i, 0)),
        ],
        out_specs=pl.BlockSpec((tm, n), lambda i: (i, 0)),
        out_shape=jax.ShapeDtypeStruct((m, n), F32),
        compiler_params=_compiler_params(("arbitrary",), 48 * 1024 * 1024),
        name="mm_resid",
    )(a, w, resid)


def _bias_table_kernel(rb_ref, idx_ref, add_ref, o_ref):
    nrel = rb_ref.shape[1]
    rb = rb_ref[...]
    hi = rb.astype(BF16)
    rest = rb - hi.astype(F32)
    mid = rest.astype(BF16)
    lo = (rest - mid.astype(F32)).astype(BF16)
    rel = lax.broadcasted_iota(jnp.int32, (nrel, idx_ref.shape[1]), 0)
    for r in range(idx_ref.shape[0]):
        onehot = jnp.where(rel == idx_ref[r:r + 1, :], 1.0, 0.0).astype(BF16)
        picked = _dot(hi, onehot) + _dot(mid, onehot) + _dot(lo, onehot)
        o_ref[:, r, :] = picked * LOG2E + add_ref[r:r + 1, :]


def _band_layout(rows, keys):
    qpos = np.arange(rows)[:, None]
    kpos = np.arange(keys)[None, :] - (keys - rows)
    qc = qpos // CHUNK
    kc = np.floor_divide(kpos, CHUNK)
    in_band = (kc <= qc) & (kc >= qc - LEFT_CHUNKS)
    idx = np.clip(qpos - kpos, -(CHUNK - 1), MAX_REL) + (CHUNK - 1)
    idx = np.where(in_band, idx, 0).astype(np.int32)
    add = np.where(in_band, 0.0, MASK_VALUE).astype(np.float32)
    return idx, add


def _bias_table(rel_bias, rows, keys):
    heads, nrel = rel_bias.shape
    nrel_pad = -(-nrel // LANES) * LANES
    rb = jnp.pad(rel_bias.astype(F32), ((0, 0), (0, nrel_pad - nrel)))
    idx, add = _band_layout(rows, keys)
    tr = SUBLANES
    return pl.pallas_call(
        _bias_table_kernel,
        grid=(rows // tr,),
        in_specs=[
            pl.BlockSpec((heads, nrel_pad), lambda i: (0, 0)),
            pl.BlockSpec((tr, keys), lambda i: (i, 0)),
            pl.BlockSpec((tr, keys), lambda i: (i, 0)),
        ],
        out_specs=pl.BlockSpec((heads, tr, keys), lambda i: (0, i, 0)),
        out_shape=jax.ShapeDtypeStruct((heads, rows, keys), F32),
        compiler_params=_compiler_params(("arbitrary",), 32 * 1024 * 1024),
        name="att_bias_table",
    )(rb, jnp.asarray(idx), jnp.asarray(add))


def _attn_kernel(q_ref, k0_ref, k1_ref, k2_ref, v0_ref, v1_ref, v2_ref, bias_ref, o_ref,
                 *, head_dim):
    qb = pl.program_id(2)
    rows = q_ref.shape[0]
    k_refs = (k0_ref, k1_ref, k2_ref)
    v_refs = (v0_ref, v1_ref, v2_ref)
    nblk = len(k_refs)
    heads = range(q_ref.shape[1] // head_dim)
    sl = [slice(hh * head_dim, (hh + 1) * head_dim) for hh in heads]
    def qk(hh):
        q = q_ref[:, sl[hh]]
        row = []
        for t in range(nblk):
            s = _dot_nt(q, k_refs[t][:, sl[hh]]) + bias_ref[hh, :, t * rows:(t + 1) * rows]
            if t < nblk - 1:
                s = jnp.where(qb >= nblk - 1 - t, s, MASK_VALUE)
            row.append(s)
        return row

    def fold(parts, op):
        m = functools.reduce(op, parts)
        return functools.reduce(op, [m[:, i:i + LANES] for i in range(0, m.shape[1], LANES)])

    ones = jnp.ones((rows, head_dim), BF16)

    def softmax(sc):
        mx = fold(sc, jnp.maximum).max(axis=-1, keepdims=True)
        return [jnp.exp2((s - mx).astype(BF16)) for s in sc]

    def pv(hh, p):
        acc = sum(_dot(p[t], jnp.concatenate([v_refs[t][:, sl[hh]], ones], axis=1))
                  for t in range(nblk))
        o_ref[:, sl[hh]] = (acc[:, :head_dim] / acc[:, head_dim:]).astype(o_ref.dtype)

    sc = qk(0)
    for hh in heads:
        nxt = qk(hh + 1) if hh + 1 < len(heads) else None
        pv(hh, softmax(sc))
        sc = nxt


def _band_attention(qkv, bias, layer, batch, seq, d, head_dim):
    m = qkv.shape[0]
    r = ATT_ROWS
    gw = ATT_HEAD_GROUP * head_dim
    n_groups = d // gw
    nqb = seq // r
    nblk = bias.shape[2] // r

    def q_map(g, b, qb):
        return (b * nqb + qb, g)

    def kv_map(off, col0):
        def f(g, b, qb):
            return (b * nqb + jnp.maximum(qb - off, 0), col0 + g)
        return f

    k_specs = [pl.BlockSpec((r, gw), kv_map(nblk - 1 - t, n_groups)) for t in range(nblk)]
    v_specs = [pl.BlockSpec((r, gw), kv_map(nblk - 1 - t, 2 * n_groups)) for t in range(nblk)]
    kern = functools.partial(_attn_kernel, head_dim=head_dim)
    return pl.pallas_call(
        kern,
        grid=(n_groups, batch, nqb),
        in_specs=[pl.BlockSpec((r, gw), q_map)] + k_specs + v_specs + [
            pl.BlockSpec((ATT_HEAD_GROUP, r, nblk * r),
                         lambda g, b, qb: (layer * n_groups + g, 0, 0),
                         pipeline_mode=pl.Buffered(1))],
        out_specs=pl.BlockSpec((r, gw), q_map),
        out_shape=jax.ShapeDtypeStruct((m, d), BF16),
        compiler_params=_compiler_params(("arbitrary", "arbitrary", "arbitrary"),
                                         48 * 1024 * 1024),
        name="band_attention",
    )(qkv, qkv, qkv, qkv, qkv, qkv, qkv, bias)


def _pool_kernel(x_ref, g_ref, w_ref, sc_ref, o_ref, ext_ref, lvl_ref, *, tiles_per_seq, windows):
    i = pl.program_id(0)
    tm, d = x_ref.shape
    halo = POOL_HALO
    n = halo + tm
    x = x_ref[...]
    h = _rms_rows(x, g_ref[...])
    first = (i % tiles_per_seq) == 0

    @pl.when(first)
    def _():
        ext_ref[0:halo, :] = jnp.zeros((halo, d), F32)

    @pl.when(jnp.logical_not(first))
    def _():
        ext_ref[0:halo, :] = ext_ref[tm:tm + halo, :]

    ext_ref[halo:halo + tm, :] = h
    pos = (i % tiles_per_seq) * tm + lax.broadcasted_iota(jnp.int32, (tm, 1), 0)
    dg = d // len(windows)
    for g, w in enumerate(windows):
        cs = slice(g * dg, (g + 1) * dg)
        levels = w.bit_length() - 1
        assert w == 1 << levels and SUBLANES * levels <= halo
        src = ext_ref.at[:, cs]
        for k in range(levels):
            lo = SUBLANES * (k + 1)
            shift = 1 << k
            summed = src[lo:n, :] + src[pl.ds(lo - shift, n - lo), :]
            if k + 1 < levels:
                dst = lvl_ref.at[k % 2]
                dst[lo:n, :] = summed
                src = dst
        acc = summed[halo - SUBLANES * levels:, :]
        hg = h[:, cs]
        cnt = jnp.minimum(pos + 1, w).astype(F32)
        pooled = acc / cnt - hg
        y = _dot(pooled.astype(BF16), w_ref[g]) * sc_ref[:, cs]
        o_ref[:, cs] = x[:, cs] + y


def _pool_mixer(x, gain, pool_w, pool_scale, seq):
    m, d = x.shape
    g, dg, _ = pool_w.shape
    halo = POOL_HALO
    kern = functools.partial(_pool_kernel, tiles_per_seq=seq // POOL_TM, windows=POOL_WINDOWS)
    return pl.pallas_call(
        kern,
        grid=(m // POOL_TM,),
        in_specs=[
            pl.BlockSpec((POOL_TM, d), lambda i: (i, 0)),
            pl.BlockSpec((1, d), lambda i: (0, 0)),
            pl.BlockSpec((g, dg, dg), lambda i: (0, 0, 0)),
            pl.BlockSpec((1, d), lambda i: (0, 0)),
        ],
        out_specs=pl.BlockSpec((POOL_TM, d), lambda i: (i, 0)),
        out_shape=jax.ShapeDtypeStruct((m, d), F32),
        scratch_shapes=[pltpu.VMEM((POOL_TM + halo, d), F32),
                        pltpu.VMEM((2, POOL_TM + halo, dg), F32)],
        compiler_params=_compiler_params(("arbitrary",), 40 * 1024 * 1024),
        name="pool_mixer",
    )(x, gain, pool_w, pool_scale)


def _mm_inv(p, q):
    return _dot(p.astype(BF16), q.astype(BF16))


def _gdn_kernel(q_ref, k_ref, v_ref, gb_ref, rows_ref, gate_ref, srows_ref, og_ref, o_ref,
                s_ref, gq_ref, ho_ref, *, n_steps, n_kheads, n_blocks, n_chunks, n_heads, dv):
    t = pl.program_id(0)
    slot_p = t % 2
    slot_s = 1 - slot_p
    hk = (jnp.minimum(t, n_steps - 1) // n_blocks) % n_kheads
    scan_block = jnp.maximum(t - 1, 0) % n_blocks

    c2 = 2 * CHUNK
    blk = c2 + CHUNK
    ri = lax.broadcasted_iota(jnp.int32, (c2, c2), 0)
    ci = lax.broadcasted_iota(jnp.int32, (c2, c2), 1)
    same = (ri // CHUNK) == (ci // CHUNK)
    causal = same & (ri >= ci)
    strict = same & (ri > ci)
    eye = (ri == ci).astype(F32)
    head0_lanes = ci < CHUNK
    lane = lax.broadcasted_iota(jnp.int32, (CHUNK, LANES), 1)
    og = og_ref[...]
    n = range(n_chunks)

    def col(gb, idx):
        return jnp.sum(jnp.where(lane == idx, gb, 0.0), axis=1, keepdims=True)

    sel_row = lax.broadcasted_iota(jnp.int32, (LANES, 4 * LANES), 0)
    sel_blk = lax.broadcasted_iota(jnp.int32, (LANES, 4 * LANES), 1) // LANES
    sel = jnp.where(sel_row == 2 * hk + (sel_blk % 2) + n_heads * (sel_blk // 2), 1.0, 0.0).astype(BF16)

    def prepare():
        r0 = [c * CHUNK for c in n]
        q = [q_ref[r:r + CHUNK, :] for r in r0]
        k = [k_ref[r:r + CHUNK, :] for r in r0]
        gb = [gb_ref[r:r + CHUNK, :] for r in r0]
        rows = [rows_ref[0, c] for c in n]
        grow = [x[0:1] for x in rows]
        brow = [x[1:2] for x in rows]
        glast = [x[2:3] for x in rows]
        gcol, bcol = [], []
        for c, g in enumerate(gb):
            if c % GDN_SELECT_EVERY == 0:
                hi = g.astype(BF16)
                rest = g - hi.astype(F32)
                mid = rest.astype(BF16)
                lo = (rest - mid.astype(F32)).astype(BF16)
                p = _dot(hi, sel) + _dot(mid, sel) + _dot(lo, sel)
                gcol.append(jnp.concatenate([p[:, 0:LANES], p[:, LANES:2 * LANES]], axis=0))
                bcol.append(jnp.concatenate([p[:, 2 * LANES:3 * LANES], p[:, 3 * LANES:]], axis=0))
            else:
                gcol.append(jnp.concatenate([col(g, 2 * hk), col(g, 2 * hk + 1)], axis=0))
                bcol.append(jnp.concatenate([col(g, n_heads + 2 * hk),
                                             col(g, n_heads + 2 * hk + 1)], axis=0))
        yield
        k2 = [jnp.concatenate([x, x], axis=0) for x in k]
        qkk = [_dot_nt(jnp.concatenate([q[i], k[i]], axis=0), k2[i]) for i in n]
        yield
        qk = [jnp.concatenate([x[:CHUNK], x[:CHUNK]], axis=0) for x in qkk]
        kk = [jnp.concatenate([x[CHUNK:], x[CHUNK:]], axis=0) for x in qkk]
        decay = [jnp.where(causal, jnp.exp(jnp.where(causal, gcol[i] - grow[i], 0.0)), 0.0)
                 for i in n]
        a = [jnp.where(strict, bcol[i] * kk[i] * decay[i], 0.0) for i in n]
        attn = [qk[i] * decay[i] for i in n]
        yield
        base = SUBLANES
        bmask = (ri // base) == (ci // base)
        a0 = [jnp.where(bmask, m, 0.0) for m in a]
        p = [_mm_inv(m, m) for m in a0]
        x = [eye - m for m in a0]
        yield
        x = [xi + _mm_inv(xi, pi) for xi, pi in zip(x, p)]
        yield
        p = [_mm_inv(pi, pi) for pi in p]
        yield
        x = [xi + _mm_inv(xi, pi) for xi, pi in zip(x, p)]
        yield
        s = base
        while s < CHUNK:
            nmask = ((ri // (2 * s)) == (ci // (2 * s))) & ((ri // s) != (ci // s))
            y = [_mm_inv(jnp.where(nmask, ai, 0.0), xi) for ai, xi in zip(a, x)]
            yield
            x = [xi - _mm_inv(xi, yi) for xi, yi in zip(x, y)]
            yield
            s *= 2
        v = [v_ref[r:r + CHUNK, :] for r in r0]
        v2 = [jnp.concatenate([m[:, :dv], m[:, dv:]], axis=0) for m in v]
        u = [_dot((x[i] * brow[i]).astype(BF16), v2[i]) for i in n]
        yield
        w = [_dot((x[i] * (brow[i] * jnp.exp(grow[i]))).astype(BF16), k2[i]) for i in n]
        yield
        wu = [jnp.concatenate([w[i], u[i]], axis=1).astype(BF16) for i in n]
        kte = [k2[i].astype(F32).T * jnp.exp(glast[i] - grow[i]) for i in n]
        lhs = [jnp.concatenate([
            jnp.where(head0_lanes, kte[i], 0.0),
            jnp.where(head0_lanes, 0.0, kte[i]),
            attn[i]], axis=0).astype(BF16) for i in n]
        yield
        r = [_dot(lhs[i], wu[i]) for i in n]
        yield
        for i in n:
            q2f = jnp.concatenate([q[i], q[i]], axis=0).astype(F32)
            qmat = q2f * jnp.exp(gcol[i]) - r[i][2 * c2:, :dv]
            omat = r[i][2 * c2:, dv:]
            gq_ref[slot_p, i] = jnp.concatenate([
                -r[i][:c2, :dv], qmat[:CHUNK], -r[i][c2:2 * c2, :dv], qmat[CHUNK:]],
                axis=0).astype(BF16)
            ho_ref[slot_p, i] = jnp.concatenate([
                r[i][:c2, dv:], omat[:CHUNK], r[i][c2:2 * c2, dv:], omat[CHUNK:]], axis=0)
        yield

    def scan():
        fresh = scan_block == 0
        states = [jnp.where(fresh, 0.0, s_ref[j]) for j in range(2)]
        for c in n:
            rows = srows_ref[0, c]
            gate = gate_ref[c * CHUNK:(c + 1) * CHUNK, :]
            outs = []
            for j in range(2):
                s = states[j]
                r = _dot(gq_ref[slot_s, c, j * blk:(j + 1) * blk, :], s.astype(BF16))
                ho = ho_ref[slot_s, c, j * blk:(j + 1) * blk, :]
                dec = jnp.exp(rows[3 + j:4 + j])
                states[j] = s * dec + r[:c2] + ho[:c2]
                o = r[c2:] + ho[c2:]
                on = o * lax.rsqrt(jnp.mean(o * o, axis=-1, keepdims=True) + EPS) * og
                outs.append(on * _silu(gate[:, j * dv:(j + 1) * dv].astype(F32)))
            o_ref[c * CHUNK:(c + 1) * CHUNK, :] = jnp.concatenate(outs, axis=1).astype(o_ref.dtype)
            yield
        s_ref[0] = states[0]
        s_ref[1] = states[1]
        yield

    def run(generators, weights):
        live = list(generators)
        while any(g is not None for g in live):
            for i, g in enumerate(live):
                for _ in range(weights[i]):
                    if live[i] is not None and next(live[i], "done") == "done":
                        live[i] = None

    @pl.when(t == 0)
    def _():
        s_ref[...] = jnp.zeros(s_ref.shape, F32)
        run([prepare()], [1])

    @pl.when((t > 0) & (t < n_steps))
    def _():
        run([scan(), prepare()], [1, 1])

    @pl.when(t == n_steps)
    def _():
        run([scan()], [1])


def _gdn_rows(gb, n_heads, n_kheads):
    m = gb.shape[0]
    nc = m // CHUNK
    rep = n_heads // n_kheads
    assert rep * CHUNK == LANES

    def stack(x):
        x = x.reshape(nc, CHUNK, n_kheads, rep).transpose(2, 0, 3, 1)
        return x.reshape(n_kheads, nc, rep * CHUNK)

    gc = gb[:, :n_heads].reshape(nc, CHUNK, n_heads)
    beta = gb[:, n_heads:2 * n_heads].reshape(nc, CHUNK, n_heads)
    last = jnp.broadcast_to(gc[:, CHUNK - 1:, :], gc.shape)
    last_k = gc[:, CHUNK - 1, :].reshape(nc, n_kheads, rep).transpose(1, 0, 2)
    per_head = [jnp.broadcast_to(last_k[:, :, r:r + 1], (n_kheads, nc, LANES)) for r in range(rep)]
    rows = [stack(gc), stack(beta), stack(last)] + per_head
    rows += [jnp.zeros_like(rows[0])] * (SUBLANES - len(rows))
    return jnp.stack(rows, axis=2)


def _gated_delta(proj, gb, rows, o_gain, batch, seq, n_kheads, n_heads, dk, dv):
    m = proj.shape[0]
    cb = GDN_CHUNKS_PER_STEP
    rb = cb * CHUNK
    ncb = seq // rb
    key_dim = n_kheads * dk
    val_dim = n_heads * dv
    vw = 2 * dv
    n_steps = batch * n_kheads * ncb
    assert n_heads == 2 * n_kheads and dk == LANES and dv == LANES

    def decode(idx):
        return (idx // (n_kheads * ncb)) * ncb + idx % ncb, (idx // ncb) % n_kheads

    def prep(col0):
        def f(t):
            row, hk = decode(jnp.minimum(t, n_steps - 1))
            return row, col0 + hk
        return f

    def lagged(col0):
        def f(t):
            row, hk = decode(jnp.maximum(t - 1, 0))
            return row, col0 + hk
        return f

    def rows_map(f):
        def g(t):
            row, hk = f(t)
            return hk, row, 0, 0
        return g

    kern = functools.partial(_gdn_kernel, n_steps=n_steps, n_kheads=n_kheads, n_blocks=ncb,
                             n_chunks=cb, n_heads=n_heads, dv=dv)
    return pl.pallas_call(
        kern,
        grid=(n_steps + 1,),
        in_specs=[
            pl.BlockSpec((rb, dk), prep(0)),
            pl.BlockSpec((rb, dk), prep(key_dim // dk)),
            pl.BlockSpec((rb, vw), prep(2 * key_dim // vw)),
            pl.BlockSpec((rb, LANES), lambda t: (prep(0)(t)[0], 0)),
            pl.BlockSpec((1, cb, SUBLANES, LANES), rows_map(prep(0))),
            pl.BlockSpec((rb, vw), lagged((2 * key_dim + val_dim) // vw)),
            pl.BlockSpec((1, cb, SUBLANES, LANES), rows_map(lagged(0))),
            pl.BlockSpec((1, dv), lambda t: (0, 0)),
        ],
        out_specs=pl.BlockSpec((rb, vw), lagged(0)),
        out_shape=jax.ShapeDtypeStruct((m, val_dim), BF16),
        scratch_shapes=[
            pltpu.VMEM((2, dk, dv), F32),
            pltpu.VMEM((2, cb, 2 * (dk + CHUNK), dv), BF16),
            pltpu.VMEM((2, cb, 2 * (dk + CHUNK), dv), F32),
        ],
        compiler_params=_compiler_params(("arbitrary",), 32 * 1024 * 1024),
        name="gated_delta",
    )(proj, proj, proj, gb, rows, proj, rows, o_gain)


def _pad_cols(w, n):
    return jnp.pad(w, ((0, 0), (0, n - w.shape[1])))


def _attention_layer(x, gain, w_qkv, q_gain, k_gain, bias, layer, w_o, batch, seq):
    d = x.shape[1]
    head_dim = d // ATT_HEADS
    qk_gain = jnp.zeros((SUBLANES, head_dim), F32)
    qk_gain = qk_gain.at[0].set(q_gain.astype(F32) * (head_dim ** -0.5 * LOG2E)).at[1].set(k_gain.astype(F32))
    qkv = _qkv_proj(x, gain, w_qkv.astype(BF16), qk_gain, head_dim)
    o = _band_attention(qkv, bias, layer, batch, seq, d, head_dim)
    return _mm_resid(o, w_o.astype(BF16), x)


def _gdn_layer(x, gain, w_in, conv_w, a_log, dt_bias, o_gain, w_o, batch, seq):
    d = x.shape[1]
    dk = d // GDN_K_HEADS
    dv = dk
    key_dim = GDN_K_HEADS * dk
    val_dim = GDN_V_HEADS * dv
    conv_ch = 2 * key_dim + val_dim
    n_main = conv_ch + val_dim
    w_in = w_in.astype(BF16)
    proj = _gdn_proj(x, gain, w_in[:, :n_main], conv_w.astype(F32), seq, key_dim, dk)
    w_ab = _pad_cols(w_in[:, n_main:], LANES)
    pad = LANES - GDN_V_HEADS
    alog = jnp.pad(a_log.astype(F32), (0, pad)).reshape(1, LANES)
    dtb = jnp.pad(dt_bias.astype(F32), (0, pad)).reshape(1, LANES)
    gb = _gdn_gates(x, gain, w_ab, alog, dtb, GDN_V_HEADS)
    rows = _gdn_rows(gb, GDN_V_HEADS, GDN_K_HEADS)
    o = _gated_delta(proj, gb, rows, o_gain.astype(F32).reshape(1, dv), batch, seq,
                     GDN_K_HEADS, GDN_V_HEADS, dk, dv)
    return _mm_resid(o, w_o.astype(BF16), x)


def _ffn_weights(w_up, conv_w, w_down):
    d_ff = conv_w.shape[-1]
    pad = -(-d_ff // SUB) * SUB - d_ff
    w_up = w_up.astype(BF16)
    wu = jnp.pad(w_up[:, :, :d_ff], ((0, 0), (0, 0), (0, pad)))
    wg = jnp.pad(w_up[:, :, d_ff:], ((0, 0), (0, 0), (0, pad)))
    cw = jnp.pad(conv_w.astype(F32), ((0, 0), (0, 0), (0, pad)))
    wd = jnp.pad(w_down.astype(BF16), ((0, 0), (0, pad), (0, 0)))
    return wu, wg, cw, wd


def kernel(x, mix_norm, ffn_norm, att_w_qkv, att_q_gain, att_k_gain, att_rel_bias, att_w_o,
           pool_w, pool_scale, gdn_w_in, gdn_conv, gdn_a_log, gdn_dt_bias, gdn_o_gain,
           gdn_w_o, ffn_w_up, ffn_conv, ffn_w_down):
    batch, seq, d = x.shape
    depth = mix_norm.shape[0]
    assert seq % TM == 0 and seq % POOL_TM == 0 and seq % ATT_ROWS == 0
    assert seq % (GDN_CHUNKS_PER_STEP * CHUNK) == 0
    xf = x.reshape(batch * seq, d).astype(F32)
    att_bias = _bias_table(att_rel_bias.reshape(-1, att_rel_bias.shape[-1]), ATT_ROWS,
                           ATT_ROWS + LEFT_CHUNKS * CHUNK)
    ffn_w = _ffn_weights(ffn_w_up, ffn_conv, ffn_w_down)
    for i in range(depth):
        kind = i % N_MIXERS
        j = i // N_MIXERS
        gain = mix_norm[i].astype(F32).reshape(1, d)
        if kind == 0:
            xf = _attention_layer(xf, gain, att_w_qkv[j], att_q_gain[j], att_k_gain[j],
                                  att_bias, j, att_w_o[j], batch, seq)
        elif kind == 1:
            xf = _pool_mixer(xf, gain, pool_w[j].astype(BF16),
                             pool_scale[j].astype(F32).reshape(1, d), seq)
        else:
            xf = _gdn_layer(xf, gain, gdn_w_in[j], gdn_conv[j], gdn_a_log[j], gdn_dt_bias[j],
                            gdn_o_gain[j], gdn_w_o[j], batch, seq)
        fgain = ffn_norm[i].astype(F32).reshape(1, d)
        xf = _conv_ffn_call(xf, fgain, *ffn_w, i, seq)
    return xf.reshape(batch, seq, d).astype(x.dtype)
```
